```python
import jax
import jax.numpy as jnp
from jax import lax
import numpy as np

D_MODEL = 4096
BATCH = 1
SEQ = 16384
DEPTH = 2
DEC_BATCH = 4
DEC_SEQ = 2048
PAST_LEN = 128

GRID_W = 64
NA_HEAD_DIM = 128
NA_WIDTH = D_MODEL // 2
NA_HEADS = NA_WIDTH // NA_HEAD_DIM
NA_ROWS = 8
NA_COLS = 16
SW_HEAD_DIM = 64
SW_WIDTH = D_MODEL // 2
SW_Q_HEADS = SW_WIDTH // SW_HEAD_DIM
SW_KV_HEADS = SW_Q_HEADS // 8
SW_KV_WIDTH = SW_KV_HEADS * SW_HEAD_DIM
SW_WINDOW = 128
SW_BLOCK = 128
ROPE_THETA = 10000.0
IN_COLS = 3 * NA_WIDTH + SW_WIDTH + 2 * SW_KV_WIDTH
IN_SPLITS = (NA_WIDTH, 2 * NA_WIDTH, 3 * NA_WIDTH, 3 * NA_WIDTH + SW_WIDTH,
             3 * NA_WIDTH + SW_WIDTH + SW_KV_WIDTH)
N_EXPERTS = 32
TOP_K = 4
D_FF_EXPERT = D_MODEL // 2
SWIGLU_LIMIT = 7.0
SWIGLU_ALPHA = 1.702
MOE_BLOCK = 128
NORM_EPS = 1e-6
NEG_INF = -1e30

kernel_name = "hybrid_natten_swa_moe_encoder"


def rms_norm(x, g):
    x32 = x.astype(jnp.float32)
    y = x32 * lax.rsqrt(jnp.mean(x32 * x32, axis=-1, keepdims=True) + NORM_EPS)
    return (y * g.astype(jnp.float32)).astype(x.dtype)


def rotary(x):
    S, hd = x.shape[1], x.shape[-1]
    inv_freq = ROPE_THETA ** (-jnp.arange(0, hd, 2, dtype=jnp.float32) / hd)
    ang = jnp.arange(S, dtype=jnp.float32)[:, None] * inv_freq[None, :]
    cos = jnp.cos(ang)[None, :, None, :]
    sin = jnp.sin(ang)[None, :, None, :]
    x1, x2 = jnp.split(x.astype(jnp.float32), 2, axis=-1)
    out = jnp.concatenate([x1 * cos - x2 * sin, x2 * cos + x1 * sin], axis=-1)
    return out.astype(x.dtype)


def neighbourhood_attention(q, k, v, rpb):
    B, S, H, hd = q.shape
    rows = S // GRID_W
    kr = min(NA_ROWS, rows)
    scale = hd ** -0.5
    qg = q.reshape(B, rows, GRID_W, H, hd)
    kg = k.reshape(B, rows, GRID_W, H, hd)
    vg = v.reshape(B, rows, GRID_W, H, hd)
    col = jnp.arange(GRID_W)
    col_start = jnp.clip(col - NA_COLS // 2, 0, GRID_W - NA_COLS)
    col_idx = col_start[:, None] + jnp.arange(NA_COLS)[None, :]
    dc_idx = col_idx - col[:, None] + (NA_COLS - 1)

    def row_block(i):
        row_start = jnp.clip(i - kr // 2, 0, rows - kr)
        k_rows = lax.dynamic_slice_in_dim(kg, row_start, kr, axis=1)
        v_rows = lax.dynamic_slice_in_dim(vg, row_start, kr, axis=1)
        k_nb = k_rows[:, :, col_idx]
        v_nb = v_rows[:, :, col_idx]
        q_i = lax.dynamic_index_in_dim(qg, i, axis=1, keepdims=False)
        s = jnp.einsum('bjhd,brjchd->bhjrc', q_i, k_nb,
                       preferred_element_type=jnp.float32) * scale
        dr_idx = row_start + jnp.arange(kr) - i + (NA_ROWS - 1)
        bias = rpb.astype(jnp.float32)[:, dr_idx][:, :, dc_idx]
        s = s + jnp.transpose(bias, (0, 2, 1, 3))[None]
        p = jax.nn.softmax(s.reshape(B, H, GRID_W, kr * NA_COLS), axis=-1)
        p = p.reshape(B, H, GRID_W, kr, NA_COLS).astype(v.dtype)
        return jnp.einsum('bhjrc,brjchd->bjhd', p, v_nb)

    out = lax.map(row_block, jnp.arange(rows))
    return jnp.transpose(out, (1, 0, 2, 3, 4)).reshape(B, S, H * hd)


def windowed_gqa(q, k, v, sinks):
    B, S, HQ, hd = q.shape
    KV = k.shape[2]
    G = HQ // KV
    nb = S // SW_BLOCK
    scale = hd ** -0.5
    qb = q.reshape(B, nb, SW_BLOCK, KV, G, hd)
    pad = ((0, 0), (SW_BLOCK, SW_BLOCK), (0, 0), (0, 0))
    kp = jnp.pad(k, pad).reshape(B, nb + 2, SW_BLOCK, KV, hd)
    vp = jnp.pad(v, pad).reshape(B, nb + 2, SW_BLOCK, KV, hd)
    k_band = jnp.concatenate([kp[:, :-2], kp[:, 1:-1], kp[:, 2:]], axis=2)
    v_band = jnp.concatenate([vp[:, :-2], vp[:, 1:-1], vp[:, 2:]], axis=2)
    s = jnp.einsum('bnqkgd,bnmkd->bnkgqm', qb, k_band,
                   preferred_element_type=jnp.float32) * scale
    qpos = jnp.arange(SW_BLOCK)
    mpos = jnp.arange(3 * SW_BLOCK)
    rel = mpos[None, :] - SW_BLOCK - qpos[:, None]
    k_abs = jnp.arange(nb)[:, None] * SW_BLOCK - SW_BLOCK + mpos[None, :]
    valid = (jnp.abs(rel) <= SW_WINDOW)[None] & ((k_abs >= 0) & (k_abs < S))[:, None, :]
    s = jnp.where(valid[None, :, None, None], s, NEG_INF)
    sk = sinks.astype(jnp.float32).reshape(KV, G)[None, None, :, :, None]
    m = jnp.maximum(jnp.max(s, axis=-1), sk)
    e = jnp.exp(s - m[..., None])
    denom = jnp.sum(e, axis=-1) + jnp.exp(sk - m)
    p = (e / denom[..., None]).astype(v.dtype)
    o = jnp.einsum('bnkgqm,bnmkd->bnqkgd', p, v_band)
    return o.reshape(B, S, HQ * hd)


def token_mixer(h, w_in, w_branch_gate, rpb, sinks, w_proj_a, w_proj_b, w_o):
    B, S, D = h.shape
    proj = h @ w_in
    qa, ka, va, qb, kb, vb = jnp.split(proj, IN_SPLITS, axis=-1)
    na = lambda t: t.reshape(B, S, NA_HEADS, NA_HEAD_DIM)
    o_a = neighbourhood_attention(na(qa), na(ka), na(va), rpb)
    qb = rotary(qb.reshape(B, S, SW_Q_HEADS, SW_HEAD_DIM))
    kb = rotary(kb.reshape(B, S, SW_KV_HEADS, SW_HEAD_DIM))
    vb = vb.reshape(B, S, SW_KV_HEADS, SW_HEAD_DIM)
    o_b = windowed_gqa(qb, kb, vb, sinks)
    gate_a, gate_b = jnp.split(jax.nn.sigmoid(h @ w_branch_gate), 2, axis=-1)
    merged = gate_a * (o_a @ w_proj_a) + gate_b * (o_b @ w_proj_b)
    return merged @ w_o


def moe_ffn(h, w_router, b_router, w_gate_up, b_gate_up, w_down, b_down):
    T, D = h.shape
    logits = (h @ w_router + b_router).astype(jnp.float32)
    top_val, top_idx = lax.top_k(logits, TOP_K)
    weights = jax.nn.softmax(top_val, axis=-1)
    n_assign = T * TOP_K
    flat_e = top_idx.reshape(-1)
    order = jnp.argsort(flat_e)
    sorted_e = flat_e[order]
    counts = jnp.bincount(flat_e, length=N_EXPERTS)
    starts = jnp.cumsum(counts) - counts
    padded = (counts + MOE_BLOCK - 1) // MOE_BLOCK * MOE_BLOCK
    pad_ends = jnp.cumsum(padded)
    pad_starts = pad_ends - padded
    rank = jnp.arange(n_assign) - starts[sorted_e]
    dest = pad_starts[sorted_e] + rank
    n_blocks = -(-n_assign // MOE_BLOCK) + N_EXPERTS
    n_rows = n_blocks * MOE_BLOCK
    row_token = jnp.full((n_rows,), T, jnp.int32).at[dest].set((order // TOP_K).astype(jnp.int32))
    row_weight = jnp.zeros((n_rows,), jnp.float32).at[dest].set(weights.reshape(-1)[order])
    block_expert = jnp.minimum(
        jnp.searchsorted(pad_ends, jnp.arange(n_blocks) * MOE_BLOCK, side='right'), N_EXPERTS - 1)
    h_pad = jnp.concatenate([h, jnp.zeros((1, D), h.dtype)], axis=0)
    xs = h_pad[row_token].reshape(n_blocks, MOE_BLOCK, D)

    def expert_block(args):
        xb, e = args
        gu = xb @ w_gate_up[e] + b_gate_up[e]
        gate, up = jnp.split(gu, 2, axis=-1)
        gate = jnp.minimum(gate, SWIGLU_LIMIT)
        up = jnp.clip(up, -SWIGLU_LIMIT, SWIGLU_LIMIT)
        glu = gate * jax.nn.sigmoid(gate * SWIGLU_ALPHA)
        return ((up + 1) * glu) @ w_down[e] + b_down[e]

    ys = lax.map(expert_block, (xs, block_expert)).reshape(n_rows, D)
    ys = ys * row_weight[:, None].astype(ys.dtype)
    return jnp.zeros((T + 1, D), ys.dtype).at[row_token].add(ys)[:T]


def encoder_trunk(x, c, w_ada, b_ada, g_mix_pre, g_mix_post, g_ffn_pre, g_ffn_post,
                  w_in, w_branch_gate, rpb, sinks, w_proj_a, w_proj_b, w_o,
                  w_router, b_router, w_gate_up, b_gate_up, w_down, b_down):
    B, S, D = x.shape
    for l in range(DEPTH):
        mod = (jax.nn.silu(c) @ w_ada[l] + b_ada[l])[:, None, :]
        sh_a, sc_a, gt_a, sh_f, sc_f, gt_f = jnp.split(mod, 6, axis=-1)
        h = rms_norm(x, g_mix_pre[l]) * (1 + sc_a) + sh_a
        y = token_mixer(h, w_in[l], w_branch_gate[l], rpb[l], sinks[l],
                        w_proj_a[l], w_proj_b[l], w_o[l])
        x = x + gt_a * rms_norm(y, g_mix_post[l])
        h = rms_norm(x, g_ffn_pre[l]) * (1 + sc_f) + sh_f
        y = moe_ffn(h.reshape(B * S, D), w_router[l], b_router[l], w_gate_up[l],
                    b_gate_up[l], w_down[l], b_down[l]).reshape(B, S, D)
        x = x + gt_f * rms_norm(y, g_ffn_post[l])
    return x


def setup_inputs(seed: int = 0) -> dict:
    key = jax.random.key(seed)
    ks = jax.random.split(key, 24)
    f32 = jnp.float32
    nrm = lambda k, shape, s: jax.random.normal(k, shape, f32) * s
    D, L, E, F = D_MODEL, DEPTH, N_EXPERTS, D_FF_EXPERT
    return {
        "x_prompt": nrm(ks[0], (BATCH, SEQ, D), 1.0),
        "x_sample": nrm(ks[1], (DEC_BATCH, DEC_SEQ, D), 1.0),
        "c_prompt": nrm(ks[2], (BATCH, D), 1.0),
        "c_sample": nrm(ks[3], (DEC_BATCH, D), 1.0),
        "w_ada": nrm(ks[4], (L, D, 6 * D), 0.5 * D ** -0.5),
        "b_ada": nrm(ks[5], (L, 6 * D), 0.02),
        "g_mix_pre": 1.0 + nrm(ks[6], (L, D), 0.05),
        "g_mix_post": 1.0 + nrm(ks[7], (L, D), 0.05),
        "g_ffn_pre": 1.0 + nrm(ks[8], (L, D), 0.05),
        "g_ffn_post": 1.0 + nrm(ks[9], (L, D), 0.05),
        "w_in": nrm(ks[10], (L, D, IN_COLS), D ** -0.5),
        "w_branch_gate": nrm(ks[11], (L, D, 2 * D), D ** -0.5),
        "rpb": nrm(ks[12], (L, NA_HEADS, 2 * NA_ROWS - 1, 2 * NA_COLS - 1), 0.5),
        "sinks": nrm(ks[13], (L, SW_Q_HEADS), 0.5),
        "w_proj_a": nrm(ks[14], (L, NA_WIDTH, D), NA_WIDTH ** -0.5),
        "w_proj_b": nrm(ks[15], (L, SW_WIDTH, D), SW_WIDTH ** -0.5),
        "w_o": nrm(ks[16], (L, D, D), D ** -0.5),
        "w_router": nrm(ks[17], (L, D, E), D ** -0.5),
        "b_router": nrm(ks[18], (L, E), 0.01),
        "w_gate_up": nrm(ks[19], (L, E, D, 2 * F), D ** -0.5),
        "b_gate_up": nrm(ks[20], (L, E, 2 * F), 0.01),
        "w_down": nrm(ks[21], (L, E, F, D), F ** -0.5),
        "b_down": nrm(ks[22], (L, E, D), 0.01),
    }


def reference(x_prompt, x_sample, c_prompt, c_sample, w_ada, b_ada, g_mix_pre, g_mix_post,
              g_ffn_pre, g_ffn_post, w_in, w_branch_gate, rpb, sinks, w_proj_a, w_proj_b, w_o,
              w_router, b_router, w_gate_up, b_gate_up, w_down, b_down):
    y_prompt = encoder_trunk(x_prompt, c_prompt, w_ada, b_ada, g_mix_pre, g_mix_post, g_ffn_pre,
                             g_ffn_post, w_in, w_branch_gate, rpb, sinks, w_proj_a, w_proj_b, w_o,
                             w_router, b_router, w_gate_up, b_gate_up, w_down, b_down)
    y_sample = encoder_trunk(x_sample, c_sample, w_ada, b_ada, g_mix_pre, g_mix_post, g_ffn_pre,
                             g_ffn_post, w_in, w_branch_gate, rpb, sinks, w_proj_a, w_proj_b, w_o,
                             w_router, b_router, w_gate_up, b_gate_up, w_down, b_down)
    return (y_prompt, y_sample)
```

```python
import functools
import math

import jax
import jax.numpy as jnp
from jax import lax
from jax.experimental import pallas as pl
from jax.experimental.pallas import tpu as pltpu

GRID_W = 64
NA_HEAD_DIM = 128
NA_ROWS = 8
NA_COLS = 16
SW_HEAD_DIM = 64
SW_GROUP = 8
SW_WINDOW = 128
SW_BLOCK = 128
ROPE_THETA = 10000.0
TOP_K = 4
SWIGLU_LIMIT = 7.0
SWIGLU_ALPHA = 1.702
NORM_EPS = 1e-6
NEG_INF = -1e30

LANES = 128
SUBLANES = 8
VMEM_LIMIT_BYTES = 56 * 1024 * 1024

F32 = jnp.float32
BF16 = jnp.bfloat16


def _params(semantics):
    return pltpu.CompilerParams(dimension_semantics=semantics,
                                vmem_limit_bytes=VMEM_LIMIT_BYTES)


def _tile(n, pref):
    t = min(n, pref)
    while n % t:
        t //= 2
    return t


def _dot(a, b):
    return jnp.dot(a, b, preferred_element_type=F32)


def _dot_nt(a, b):
    return lax.dot_general(a, b, (((1,), (1,)), ((), ())), preferred_element_type=F32)


class _Seqs:
    def __init__(self, s0, nb1, s1):
        self.s0, self.nb1, self.s1 = s0, nb1, s1
        self.total = s0 + nb1 * s1

    def seq_of_tile(self, i, tile):
        n0 = self.s0 // tile
        per = self.s1 // tile
        return jnp.where(i < n0, 0, 1 + (i - n0) // per)

    def pos_tile(self, i, tile):
        n0 = self.s0 // tile
        per = self.s1 // tile
        return jnp.where(i < n0, i, (i - n0) % per)


def _adaln_kernel(c_ref, w_ref, b_ref, o_ref):
    c = c_ref[...]
    a = (c * jax.nn.sigmoid(c)).astype(BF16)
    o_ref[0] = _dot(a, w_ref[0].astype(BF16)) + b_ref[0]


def _adaln(c8, w_ada, b_ada):
    depth, d, n = w_ada.shape
    tn = _tile(n, 512)
    return pl.pallas_call(
        _adaln_kernel,
        grid=(depth, n // tn),
        in_specs=[
            pl.BlockSpec((SUBLANES, d), lambda l, j: (0, 0)),
            pl.BlockSpec((1, d, tn), lambda l, j: (l, 0, j)),
            pl.BlockSpec((1, 1, tn), lambda l, j: (l, 0, j)),
        ],
        out_specs=pl.BlockSpec((1, SUBLANES, tn), lambda l, j: (l, 0, j)),
        out_shape=jax.ShapeDtypeStruct((depth, SUBLANES, n), F32),
        compiler_params=_params(("arbitrary", "arbitrary")),
        name="adaln",
    )(c8, w_ada, b_ada.reshape(depth, 1, n))


def _rope_table_kernel(inv_ref, cos_ref, sin_ref):
    tm = cos_ref.shape[0]
    base = pl.program_id(0) * tm
    pos = (base + lax.broadcasted_iota(jnp.int32, (tm, LANES), 0)).astype(F32)
    ang = pos * inv_ref[...]
    lane = lax.broadcasted_iota(jnp.int32, (tm, LANES), 1)
    sign = jnp.where(lane % SW_HEAD_DIM < SW_HEAD_DIM // 2, -1.0, 1.0)
    cos_ref[...] = jnp.cos(ang)
    sin_ref[...] = jnp.sin(ang) * sign


def _rope_tables(s_max):
    half = SW_HEAD_DIM // 2
    inv_freq = ROPE_THETA ** (-jnp.arange(0, SW_HEAD_DIM, 2, dtype=F32) / SW_HEAD_DIM)
    inv_lane = jnp.tile(inv_freq, LANES // half).reshape(1, LANES)
    tm = _tile(s_max, 512)
    return pl.pallas_call(
        _rope_table_kernel,
        grid=(s_max // tm,),
        in_specs=[pl.BlockSpec((1, LANES), lambda i: (0, 0))],
        out_specs=[pl.BlockSpec((tm, LANES), lambda i: (i, 0))] * 2,
        out_shape=[jax.ShapeDtypeStruct((s_max, LANES), F32)] * 2,
        compiler_params=_params(("arbitrary",)),
        name="rope_tables",
    )(inv_lane)


def _rpb_expand_kernel(r_ref, s_ref, m_ref, o_ref):
    o_ref[...] = jnp.dot(r_ref[...], s_ref[...], preferred_element_type=F32,
                         precision=lax.Precision.HIGHEST) + m_ref[...]


def _natten_bias(rpb_l):
    heads = rpb_l.shape[0]
    nr, nc = 2 * NA_ROWS - 1, 2 * NA_COLS - 1
    nc_pad = 32
    col = jnp.arange(GRID_W)
    col_start = jnp.clip(col - NA_COLS // 2, 0, GRID_W - NA_COLS)
    cc = jnp.arange(GRID_W)[None, :]
    in_win = (cc >= col_start[:, None]) & (cc < col_start[:, None] + NA_COLS)
    dc = cc - col[:, None] + (NA_COLS - 1)
    shift = ((dc[None] == jnp.arange(nc_pad)[:, None, None]) & in_win[None]).astype(F32)
    shift = shift.reshape(nc_pad, GRID_W * GRID_W)
    negmask = jnp.where(in_win, 0.0, NEG_INF).astype(F32).reshape(1, GRID_W * GRID_W)
    rows = heads * nr
    rows_pad = -(-rows // SUBLANES) * SUBLANES
    r2 = jnp.zeros((rows_pad, nc_pad), F32).at[:rows, :nc].set(rpb_l.reshape(rows, nc).astype(F32))
    tiles = pl.pallas_call(
        _rpb_expand_kernel,
        out_shape=jax.ShapeDtypeStruct((rows_pad, GRID_W * GRID_W), F32),
        name="rpb_expand",
    )(r2, shift, negmask)
    tiles = tiles[:rows].reshape(heads, nr, GRID_W, GRID_W)
    sel = jnp.arange(NA_ROWS)[:, None] + jnp.arange(NA_ROWS)[None, :]
    var = tiles[:, sel]
    var = jnp.transpose(var, (0, 1, 3, 2, 4))
    return var.reshape(heads, NA_ROWS, GRID_W, NA_ROWS * GRID_W)


def _norm_mod(x, g, sc, sh):
    ms = jnp.mean(x * x, axis=-1, keepdims=True)
    y = x * lax.rsqrt(ms + NORM_EPS) * g
    return y * (1.0 + sc) + sh


def _inproj_kernel(x_ref, g_ref, sc_ref, sh_ref, w_ref, h_ref, o_ref, h_sc):
    @pl.when(pl.program_id(1) == 0)
    def _():
        h = _norm_mod(x_ref[...], g_ref[...], sc_ref[0], sh_ref[0]).astype(BF16)
        h_sc[...] = h
        h_ref[...] = h

    o_ref[...] = _dot(h_sc[...], w_ref[...]).astype(BF16)


def _inproj(x, g, mod_rows, w_bf, seqs, sc_idx, sh_idx):
    t, d = x.shape
    n = w_bf.shape[1]
    tm = _tile(seqs.s1, 512)
    tn = _tile(n, 512)
    mod_spec = lambda which: pl.BlockSpec(
        (1, 1, d), lambda i, j: (seqs.seq_of_tile(i, tm) * 6 + which, 0, 0))
    return pl.pallas_call(
        _inproj_kernel,
        grid=(t // tm, n // tn),
        in_specs=[
            pl.BlockSpec((tm, d), lambda i, j: (i, 0)),
            pl.BlockSpec((1, d), lambda i, j: (0, 0)),
            mod_spec(sc_idx),
            mod_spec(sh_idx),
            pl.BlockSpec((d, tn), lambda i, j: (0, j)),
        ],
        out_specs=[
            pl.BlockSpec((tm, d), lambda i, j: (i, 0)),
            pl.BlockSpec((tm, tn), lambda i, j: (i, j)),
        ],
        out_shape=[jax.ShapeDtypeStruct((t, d), BF16), jax.ShapeDtypeStruct((t, n), BF16)],
        scratch_shapes=[pltpu.VMEM((tm, d), BF16)],
        compiler_params=_params(("arbitrary", "arbitrary")),
        name="inproj",
    )(x, g.reshape(1, d), mod_rows, mod_rows, w_bf)


def _rope_kernel(q_ref, k_ref, cos_ref, sin_ref, qo_ref, ko_ref, *, q_scale):
    cos = cos_ref[...]
    sin = sin_ref[...]
    lane = lax.broadcasted_iota(jnp.int32, cos.shape, 1)
    first_half = lane % SW_HEAD_DIM < SW_HEAD_DIM // 2

    def rot(x):
        partner = jnp.where(first_half,
                            pltpu.roll(x, LANES - SW_HEAD_DIM // 2, 1),
                            pltpu.roll(x, SW_HEAD_DIM // 2, 1))
        return x * cos + partner * sin

    for c in range(q_ref.shape[1] // LANES):
        sl = slice(c * LANES, (c + 1) * LANES)
        qo_ref[:, sl] = (rot(q_ref[:, sl].astype(F32)) * q_scale).astype(BF16)
    for c in range(k_ref.shape[1] // LANES):
        sl = slice(c * LANES, (c + 1) * LANES)
        ko_ref[:, sl] = rot(k_ref[:, sl].astype(F32)).astype(BF16)


def _rope(proj, cos_t, sin_t, seqs, q_col, q_w, k_col, k_w):
    t = proj.shape[0]
    tm = _tile(seqs.s1, 512)
    tab = pl.BlockSpec((tm, LANES), lambda i: (seqs.pos_tile(i, tm), 0))
    return pl.pallas_call(
        functools.partial(_rope_kernel, q_scale=SW_HEAD_DIM ** -0.5),
        grid=(t // tm,),
        in_specs=[
            pl.BlockSpec((tm, q_w), lambda i: (i, q_col // q_w)),
            pl.BlockSpec((tm, k_w), lambda i: (i, k_col // k_w)),
            tab, tab,
        ],
        out_specs=[pl.BlockSpec((tm, q_w), lambda i: (i, 0)),
                   pl.BlockSpec((tm, k_w), lambda i: (i, 0))],
        out_shape=[jax.ShapeDtypeStruct((t, q_w), BF16), jax.ShapeDtypeStruct((t, k_w), BF16)],
        compiler_params=_params(("arbitrary",)),
        name="rope",
    )(proj, proj, cos_t, sin_t)


def _natten_kernel(q_ref, k_ref, v_ref, b_ref, o_ref, *, rows, rows_per_step):
    kr = min(NA_ROWS, rows)
    scale = NA_HEAD_DIM ** -0.5
    base = pl.program_id(2) * rows_per_step

    def one_row(rr, carry):
        i = base + rr
        row_start = jnp.clip(i - kr // 2, 0, rows - kr)
        off = row_start - i + (NA_ROWS - 1)
        q = q_ref[pl.ds(pl.multiple_of(rr * GRID_W, GRID_W), GRID_W), :]
        kstart = pl.multiple_of(row_start * GRID_W, GRID_W)
        ks = k_ref[pl.ds(kstart, kr * GRID_W), :]
        vs = v_ref[pl.ds(kstart, kr * GRID_W), :]
        s = _dot_nt(q, ks) * scale + b_ref[0, off]
        m = jnp.max(s, axis=-1, keepdims=True)
        e = jnp.exp(s - m)
        l = jnp.sum(e, axis=-1, keepdims=True)
        o = _dot(e.astype(BF16), vs) / l
        o_ref[pl.ds(pl.multiple_of(rr * GRID_W, GRID_W), GRID_W), :] = o.astype(BF16)
        return carry

    lax.fori_loop(0, rows_per_step, one_row, 0)


def _natten(proj, bias, tok0, n_seq, s, width, heads):
    rows = s // GRID_W
    assert rows >= NA_ROWS
    rps = _tile(rows, 16)
    tq = rps * GRID_W
    nh = width // NA_HEAD_DIM
    qb0 = tok0 // tq
    sb0 = tok0 // s
    return pl.pallas_call(
        functools.partial(_natten_kernel, rows=rows, rows_per_step=rps),
        grid=(n_seq, heads, rows // rps),
        in_specs=[
            pl.BlockSpec((tq, NA_HEAD_DIM), lambda b, h, r: (qb0 + b * (rows // rps) + r, h)),
            pl.BlockSpec((s, NA_HEAD_DIM), lambda b, h, r: (sb0 + b, nh + h)),
            pl.BlockSpec((s, NA_HEAD_DIM), lambda b, h, r: (sb0 + b, 2 * nh + h)),
            pl.BlockSpec((1, NA_ROWS, GRID_W, NA_ROWS * GRID_W), lambda b, h, r: (h, 0, 0, 0)),
        ],
        out_specs=pl.BlockSpec((tq, NA_HEAD_DIM), lambda b, h, r: (b * (rows // rps) + r, h)),
        out_shape=jax.ShapeDtypeStruct((n_seq * s, width), BF16),
        compiler_params=_params(("arbitrary", "arbitrary", "arbitrary")),
        name="natten",
    )(proj, proj, proj, bias)


def _swa_kernel(sink_ref, q_ref, kp_ref, kc_ref, kn_ref, vp_ref, vc_ref, vn_ref, o_ref,
                *, seqs, kv_heads):
    n = pl.program_id(0)
    pos_blk = seqs.pos_tile(n, SW_BLOCK)
    n_blk = jnp.where(n < seqs.s0 // SW_BLOCK, seqs.s0 // SW_BLOCK, seqs.s1 // SW_BLOCK)
    is_first = pos_blk == 0
    is_last = pos_blk == n_blk - 1

    band = 3 * SW_BLOCK
    qpos = lax.broadcasted_iota(jnp.int32, (SW_BLOCK, band), 0)
    mpos = lax.broadcasted_iota(jnp.int32, (SW_BLOCK, band), 1)
    rel = mpos - SW_BLOCK - qpos
    valid = (jnp.abs(rel) <= SW_WINDOW)
    valid = valid & ((mpos >= SW_BLOCK) | jnp.logical_not(is_first))
    valid = valid & ((mpos < 2 * SW_BLOCK) | jnp.logical_not(is_last))

    kb = jnp.concatenate([kp_ref[...], kc_ref[...], kn_ref[...]], axis=0).astype(F32)
    vb = jnp.concatenate([vp_ref[...], vc_ref[...], vn_ref[...]], axis=0).astype(F32)
    lane = lax.broadcasted_iota(jnp.int32, (band, LANES), 1)
    low = lane < SW_HEAD_DIM
    qlane_low = lax.broadcasted_iota(jnp.int32, (SW_BLOCK, LANES), 1) < SW_HEAD_DIM
    olane_low = qlane_low

    for k in range(kv_heads):
        pair = slice((k // 2) * LANES, (k // 2 + 1) * LANES)
        kpair, vpair = kb[:, pair], vb[:, pair]
        kroll = pltpu.roll(kpair, SW_HEAD_DIM, 1)
        vroll = pltpu.roll(vpair, SW_HEAD_DIM, 1)
        if k % 2 == 0:
            k2 = jnp.where(low, kpair, kroll)
            v2 = jnp.where(low, vpair, vroll)
        else:
            k2 = jnp.where(low, kroll, kpair)
            v2 = jnp.where(low, vroll, vpair)
        k2 = k2.astype(BF16)
        v2 = v2.astype(BF16)

        qs = []
        for g in range(SW_GROUP):
            hq = k * SW_GROUP + g
            q2 = q_ref[:, (hq // 2) * LANES:(hq // 2 + 1) * LANES]
            keep = qlane_low if hq % 2 == 0 else jnp.logical_not(qlane_low)
            qs.append(jnp.where(keep, q2, jnp.zeros_like(q2)))
        qm = jnp.concatenate(qs, axis=0)
        s = _dot_nt(qm, k2).reshape(SW_GROUP, SW_BLOCK, band)
        s = jnp.where(valid[None], s, NEG_INF)
        sink = jnp.stack([jnp.full((1, 1), sink_ref[k * SW_GROUP + g], F32)
                          for g in range(SW_GROUP)], axis=0)
        m = jnp.maximum(jnp.max(s, axis=-1, keepdims=True), sink)
        e = jnp.exp(s - m)
        denom = jnp.sum(e, axis=-1, keepdims=True) + jnp.exp(sink - m)
        o = _dot(e.reshape(SW_GROUP * SW_BLOCK, band).astype(BF16), v2)
        o = o.reshape(SW_GROUP, SW_BLOCK, LANES) / denom
        for pp in range(SW_GROUP // 2):
            o2 = jnp.where(olane_low, o[2 * pp], o[2 * pp + 1])
            c0 = (k * SW_GROUP // 2 + pp) * LANES
            o_ref[:, c0:c0 + LANES] = o2.astype(BF16)


def _swa(qr, kr, proj, sinks_l, seqs, v_col, q_w, kv_w):
    t = qr.shape[0]
    nblk = t // SW_BLOCK
    kv_heads = kv_w // SW_HEAD_DIM
    assert kv_heads % 2 == 0 and kv_w % LANES == 0

    def prev(n, s):
        return jnp.maximum(n - 1, 0)

    def nxt(n, s):
        return jnp.minimum(n + 1, nblk - 1)

    vcb = v_col // kv_w
    grid_spec = pltpu.PrefetchScalarGridSpec(
        num_scalar_prefetch=1,
        grid=(nblk,),
        in_specs=[
            pl.BlockSpec((SW_BLOCK, q_w), lambda n, s: (n, 0)),
            pl.BlockSpec((SW_BLOCK, kv_w), lambda n, s: (prev(n, s), 0)),
            pl.BlockSpec((SW_BLOCK, kv_w), lambda n, s: (n, 0)),
            pl.BlockSpec((SW_BLOCK, kv_w), lambda n, s: (nxt(n, s), 0)),
            pl.BlockSpec((SW_BLOCK, kv_w), lambda n, s: (prev(n, s), vcb)),
            pl.BlockSpec((SW_BLOCK, kv_w), lambda n, s: (n, vcb)),
            pl.BlockSpec((SW_BLOCK, kv_w), lambda n, s: (nxt(n, s), vcb)),
        ],
        out_specs=pl.BlockSpec((SW_BLOCK, q_w), lambda n, s: (n, 0)),
    )
    return pl.pallas_call(
        functools.partial(_swa_kernel, seqs=seqs, kv_heads=kv_heads),
        grid_spec=grid_spec,
        out_shape=jax.ShapeDtypeStruct((t, q_w), BF16),
        compiler_params=_params(("arbitrary",)),
        name="swa",
    )(sinks_l.astype(F32), qr, kr, kr, kr, proj, proj, proj)


def _merge_kernel(h_ref, oa_ref, ob_ref, wga_ref, wgb_ref, wa_ref, wb_ref, o_ref):
    h = h_ref[...]
    ga = jax.nn.sigmoid(_dot(h, wga_ref[...]))
    gb = jax.nn.sigmoid(_dot(h, wgb_ref[...]))
    m = ga * _dot(oa_ref[...], wa_ref[...]) + gb * _dot(ob_ref[...], wb_ref[...])
    o_ref[...] = m.astype(BF16)


def _merge(h, oa, ob, wg_bf, wa_bf, wb_bf):
    t, d = h.shape
    wa_w, wb_w = oa.shape[1], ob.shape[1]
    tm = _tile(t, 512)
    tn = _tile(d, 512)
    nj = d // tn
    return pl.pallas_call(
        _merge_kernel,
        grid=(t // tm, nj),
        in_specs=[
            pl.BlockSpec((tm, d), lambda i, j: (i, 0)),
            pl.BlockSpec((tm, wa_w), lambda i, j: (i, 0)),
            pl.BlockSpec((tm, wb_w), lambda i, j: (i, 0)),
            pl.BlockSpec((d, tn), lambda i, j: (0, j)),
            pl.BlockSpec((d, tn), lambda i, j: (0, nj + j)),
            pl.BlockSpec((wa_w, tn), lambda i, j: (0, j)),
            pl.BlockSpec((wb_w, tn), lambda i, j: (0, j)),
        ],
        out_specs=pl.BlockSpec((tm, tn), lambda i, j: (i, j)),
        out_shape=jax.ShapeDtypeStruct((t, d), BF16),
        compiler_params=_params(("arbitrary", "arbitrary")),
        name="merge",
    )(h, oa, ob, wg_bf, wg_bf, wa_bf, wb_bf)


def _matmul_kernel(x_ref, w_ref, o_ref):
    o_ref[...] = _dot(x_ref[...], w_ref[...]).astype(o_ref.dtype)


def _matmul(x, w, tm_pref=512, tn_pref=1024):
    t, k = x.shape
    n = w.shape[1]
    tm, tn = _tile(t, tm_pref), _tile(n, tn_pref)
    return pl.pallas_call(
        _matmul_kernel,
        grid=(t // tm, n // tn),
        in_specs=[pl.BlockSpec((tm, k), lambda i, j: (i, 0)),
                  pl.BlockSpec((k, tn), lambda i, j: (0, j))],
        out_specs=pl.BlockSpec((tm, tn), lambda i, j: (i, j)),
        out_shape=jax.ShapeDtypeStruct((t, n), BF16),
        compiler_params=_params(("arbitrary", "arbitrary")),
        name="matmul",
    )(x, w)


def _rms(x, g):
    ms = jnp.mean(x * x, axis=-1, keepdims=True)
    return x * lax.rsqrt(ms + NORM_EPS) * g


def _postmix_kernel(x_ref, y_ref, gp_ref, gt_ref, gf_ref, sc_ref, sh_ref, wr_ref, br_ref,
                    x1_ref, h2_ref, idx_ref, wt_ref, *, n_experts):
    x1 = x_ref[...] + gt_ref[0] * _rms(y_ref[...].astype(F32), gp_ref[...])
    x1_ref[...] = x1
    h2 = _rms(x1, gf_ref[...]) * (1.0 + sc_ref[0]) + sh_ref[0]
    h2_ref[...] = h2
    logits = jnp.dot(h2, wr_ref[...], preferred_element_type=F32,
                     precision=lax.Precision.HIGHEST) + br_ref[...]
    lane = lax.broadcasted_iota(jnp.int32, logits.shape, 1)
    work = jnp.where(lane < n_experts, logits, -jnp.inf)
    idx_out = jnp.zeros(logits.shape, jnp.int32)
    val_out = jnp.zeros(logits.shape, F32)
    vals = []
    for k in range(TOP_K):
        m = jnp.max(work, axis=-1, keepdims=True)
        sel = jnp.min(jnp.where(work == m, lane, LANES), axis=-1, keepdims=True)
        idx_out = jnp.where(lane == k, sel, idx_out)
        vals.append(m)
        work = jnp.where(lane == sel, -jnp.inf, work)
    es = [jnp.exp(v - vals[0]) for v in vals]
    tot = es[0]
    for e in es[1:]:
        tot = tot + e
    for k in range(TOP_K):
        val_out = jnp.where(lane == k, es[k] / tot, val_out)
    idx_ref[...] = idx_out
    wt_ref[...] = val_out


def _postmix(x, y, g_post, g_ffn, mod_rows, w_router, b_router, seqs, gt_idx, sc_idx, sh_idx):
    t, d = x.shape
    e = w_router.shape[1]
    tm = _tile(seqs.s1, 256)
    wr = jnp.zeros((d, LANES), F32).at[:, :e].set(w_router)
    br = jnp.zeros((1, LANES), F32).at[0, :e].set(b_router)
    mod_spec = lambda which: pl.BlockSpec(
        (1, 1, d), lambda i: (seqs.seq_of_tile(i, tm) * 6 + which, 0, 0))
    row = pl.BlockSpec((tm, d), lambda i: (i, 0))
    vec = pl.BlockSpec((1, d), lambda i: (0, 0))
    small = pl.BlockSpec((tm, LANES), lambda i: (i, 0))
    return pl.pallas_call(
        functools.partial(_postmix_kernel, n_experts=e),
        grid=(t // tm,),
        in_specs=[row, row, vec, mod_spec(gt_idx), vec, mod_spec(sc_idx), mod_spec(sh_idx),
                  pl.BlockSpec((d, LANES), lambda i: (0, 0)),
                  pl.BlockSpec((1, LANES), lambda i: (0, 0))],
        out_specs=[row, row, small, small],
        out_shape=[jax.ShapeDtypeStruct((t, d), F32), jax.ShapeDtypeStruct((t, d), F32),
                   jax.ShapeDtypeStruct((t, LANES), jnp.int32),
                   jax.ShapeDtypeStruct((t, LANES), F32)],
        compiler_params=_params(("arbitrary",)),
        name="postmix",
    )(x, y, g_post.reshape(1, d), mod_rows, g_ffn.reshape(1, d), mod_rows, mod_rows, wr, br)


def _rank_kernel(idx_ref, rank_ref, cnt_ref, carry):
    @pl.when(pl.program_id(0) == 0)
    def _():
        carry[...] = jnp.zeros_like(carry)

    idx = idx_ref[...]
    tb = idx.shape[0]
    lane = lax.broadcasted_iota(jnp.int32, idx.shape, 1)
    hits = [lane == idx[:, k:k + 1] for k in range(TOP_K)]
    onehot = jnp.zeros(idx.shape, F32)
    for hit in hits:
        onehot = onehot + hit.astype(F32)
    r = lax.broadcasted_iota(jnp.int32, (tb, tb), 0)
    c = lax.broadcasted_iota(jnp.int32, (tb, tb), 1)
    tri = (c < r).astype(BF16)
    before = _dot(tri, onehot.astype(BF16)) + carry[...]
    rank = jnp.zeros(idx.shape, F32)
    for k, hit in enumerate(hits):
        rk = jnp.sum(jnp.where(hit, before, 0.0), axis=-1, keepdims=True)
        rank = jnp.where(lane == k, rk, rank)
    rank_ref[...] = rank.astype(jnp.int32)
    carry[...] = carry[...] + jnp.sum(onehot, axis=0, keepdims=True)
    cnt_ref[...] = jnp.broadcast_to(carry[...], cnt_ref.shape).astype(jnp.int32)


def _ranks(idx):
    t = idx.shape[0]
    tb = _tile(t, 256)
    return pl.pallas_call(
        _rank_kernel,
        grid=(t // tb,),
        in_specs=[pl.BlockSpec((tb, LANES), lambda i: (i, 0))],
        out_specs=[pl.BlockSpec((tb, LANES), lambda i: (i, 0)),
                   pl.BlockSpec((SUBLANES, LANES), lambda i: (0, 0))],
        out_shape=[jax.ShapeDtypeStruct((t, LANES), jnp.int32),
                   jax.ShapeDtypeStruct((SUBLANES, LANES), jnp.int32)],
        scratch_shapes=[pltpu.VMEM((1, LANES), F32)],
        compiler_params=_params(("arbitrary",)),
        name="ranks",
    )(idx)


def _row_copy(src_hbm, row, dst, slot, sem):
    return pltpu.make_async_copy(src_hbm.at[pl.ds(row, 1), :], dst.at[pl.ds(slot, 1), :], sem)


def _dispatch_kernel(tok_ref, used_ref, h_hbm, o_ref, buf, sem):
    r = pl.program_id(0)
    tg = buf.shape[0]

    @pl.when(r < used_ref[0])
    def _():
        def start(i, c):
            _row_copy(h_hbm, tok_ref[0, 0, i], buf, i, sem).start()
            return c

        lax.fori_loop(0, tg, start, 0)

        def wait(i, c):
            _row_copy(h_hbm, 0, buf, i, sem).wait()
            return c

        lax.fori_loop(0, tg, wait, 0)
        o_ref[...] = buf[...].astype(BF16)

    @pl.when(r >= used_ref[0])
    def _():
        o_ref[...] = jnp.zeros_like(o_ref)


def _dispatch(h2, row_token, tiles_used, tg):
    n_rows = row_token.shape[0]
    d = h2.shape[1]
    nt = n_rows // tg
    grid_spec = pltpu.PrefetchScalarGridSpec(
        num_scalar_prefetch=0,
        grid=(nt,),
        in_specs=[
            pl.BlockSpec((1, 1, tg), lambda r: (r, 0, 0), memory_space=pltpu.SMEM),
            pl.BlockSpec(memory_space=pltpu.SMEM),
            pl.BlockSpec(memory_space=pl.ANY),
        ],
        out_specs=pl.BlockSpec((tg, d), lambda r: (r, 0)),
        scratch_shapes=[pltpu.VMEM((tg, d), F32), pltpu.SemaphoreType.DMA(())],
    )
    return pl.pallas_call(
        _dispatch_kernel,
        grid_spec=grid_spec,
        out_shape=jax.ShapeDtypeStruct((n_rows, d), BF16),
        compiler_params=_params(("arbitrary",)),
        name="dispatch",
    )(row_token.reshape(nt, 1, tg), tiles_used, h2)


def _gateup_kernel(e_ref, src_ref, used_ref, x_ref, wg_ref, wu_ref, bg_ref, bu_ref, o_ref):
    r = pl.program_id(1)

    @pl.when(r < used_ref[0])
    def _():
        x = x_ref[...]
        gate = _dot(x, wg_ref[0]) + bg_ref[0]
        up = _dot(x, wu_ref[0]) + bu_ref[0]
        gate = jnp.minimum(gate, SWIGLU_LIMIT)
        up = jnp.clip(up, -SWIGLU_LIMIT, SWIGLU_LIMIT)
        glu = gate * jax.nn.sigmoid(gate * SWIGLU_ALPHA)
        o_ref[...] = ((up + 1.0) * glu).astype(BF16)

    @pl.when(r >= used_ref[0])
    def _():
        o_ref[...] = jnp.zeros_like(o_ref)


def _gateup(xs, wgu_bf, bgu, tile_expert, tile_src, tiles_used, tm):
    n_rows, d = xs.shape
    n_exp, _, two_f = wgu_bf.shape
    f = two_f // 2
    tn = _tile(f, 512)
    nj = f // tn
    nt = n_rows // tm
    b3 = bgu.reshape(n_exp, 1, two_f)
    grid_spec = pltpu.PrefetchScalarGridSpec(
        num_scalar_prefetch=3,
        grid=(nj, nt),
        in_specs=[
            pl.BlockSpec((tm, d), lambda j, r, e, s, u: (s[r], 0)),
            pl.BlockSpec((1, d, tn), lambda j, r, e, s, u: (e[r], 0, j)),
            pl.BlockSpec((1, d, tn), lambda j, r, e, s, u: (e[r], 0, nj + j)),
            pl.BlockSpec((1, 1, tn), lambda j, r, e, s, u: (e[r], 0, j)),
            pl.BlockSpec((1, 1, tn), lambda j, r, e, s, u: (e[r], 0, nj + j)),
        ],
        out_specs=pl.BlockSpec((tm, tn), lambda j, r, e, s, u: (r, j)),
    )
    return pl.pallas_call(
        _gateup_kernel,
        grid_spec=grid_spec,
        out_shape=jax.ShapeDtypeStruct((n_rows, f), BF16),
        compiler_params=_params(("arbitrary", "arbitrary")),
        name="gateup",
    )(tile_expert, tile_src, tiles_used, xs, wgu_bf, wgu_bf, b3, b3)


def _down_kernel(e_ref, src_ref, used_ref, a_ref, w_ref, b_ref, o_ref):
    r = pl.program_id(1)

    @pl.when(r < used_ref[0])
    def _():
        o_ref[...] = _dot(a_ref[...], w_ref[0]) + b_ref[0]

    @pl.when(r >= used_ref[0])
    def _():
        o_ref[...] = jnp.zeros_like(o_ref)


def _down(a, wd_bf, bd, tile_expert, tile_src, tiles_used, tm):
    n_rows, f = a.shape
    n_exp, _, d = wd_bf.shape
    tn = _tile(d, 1024)
    nt = n_rows // tm
    grid_spec = pltpu.PrefetchScalarGridSpec(
        num_scalar_prefetch=3,
        grid=(d // tn, nt),
        in_specs=[
            pl.BlockSpec((tm, f), lambda j, r, e, s, u: (s[r], 0)),
            pl.BlockSpec((1, f, tn), lambda j, r, e, s, u: (e[r], 0, j)),
            pl.BlockSpec((1, 1, tn), lambda j, r, e, s, u: (e[r], 0, j)),
        ],
        out_specs=pl.BlockSpec((tm, tn), lambda j, r, e, s, u: (r, j)),
    )
    return pl.pallas_call(
        _down_kernel,
        grid_spec=grid_spec,
        out_shape=jax.ShapeDtypeStruct((n_rows, d), F32),
        compiler_params=_params(("arbitrary", "arbitrary")),
        name="down",
    )(tile_expert, tile_src, tiles_used, a, wd_bf, bd.reshape(n_exp, 1, d))


def _combine_kernel(dest_ref, ys_hbm, wt_ref, x_ref, g_ref, gt_ref, o_ref, buf, sem):
    tc = x_ref.shape[0]

    def start(i, c):
        for k in range(TOP_K):
            _row_copy(ys_hbm, dest_ref[0, k, i], buf.at[k], i, sem).start()
        return c

    lax.fori_loop(0, tc, start, 0)

    def wait(i, c):
        for k in range(TOP_K):
            _row_copy(ys_hbm, 0, buf.at[k], i, sem).wait()
        return c

    lax.fori_loop(0, tc, wait, 0)
    wt = wt_ref[...]
    y = wt[:, 0:1] * buf[0]
    for k in range(1, TOP_K):
        y = y + wt[:, k:k + 1] * buf[k]
    o_ref[...] = x_ref[...] + gt_ref[0] * _rms(y, g_ref[...])


def _combine(ys, dest, wt, x1, g_post, mod_rows, seqs, gt_idx):
    t, d = x1.shape
    tc = _tile(seqs.s1, 128)
    nt = t // tc
    dest_t = dest.reshape(nt, tc, TOP_K).transpose(0, 2, 1)
    grid_spec = pltpu.PrefetchScalarGridSpec(
        num_scalar_prefetch=0,
        grid=(nt,),
        in_specs=[
            pl.BlockSpec((1, TOP_K, tc), lambda i: (i, 0, 0), memory_space=pltpu.SMEM),
            pl.BlockSpec(memory_space=pl.ANY),
            pl.BlockSpec((tc, LANES), lambda i: (i, 0)),
            pl.BlockSpec((tc, d), lambda i: (i, 0)),
            pl.BlockSpec((1, d), lambda i: (0, 0)),
            pl.BlockSpec((1, 1, d), lambda i: (seqs.seq_of_tile(i, tc) * 6 + gt_idx, 0, 0)),
        ],
        out_specs=pl.BlockSpec((tc, d), lambda i: (i, 0)),
        scratch_shapes=[pltpu.VMEM((TOP_K, tc, d), F32), pltpu.SemaphoreType.DMA(())],
    )
    return pl.pallas_call(
        _combine_kernel,
        grid_spec=grid_spec,
        out_shape=jax.ShapeDtypeStruct((t, d), F32),
        compiler_params=_params(("arbitrary",)),
        name="combine",
    )(dest_t, ys, wt, x1, g_post.reshape(1, d), mod_rows)


def _moe(h2, idx, wt, wgu_bf, bgu, wd_bf, bd, tm):
    t, d = h2.shape
    n_exp = wgu_bf.shape[0]
    rank, cnt = _ranks(idx)
    counts = cnt[0, :n_exp]
    padded = (counts + tm - 1) // tm * tm
    pad_ends = jnp.cumsum(padded)
    pad_starts = pad_ends - padded
    n_assign = t * TOP_K
    nt = -(-n_assign // tm) + n_exp
    n_rows = nt * tm
    top_idx = idx[:, :TOP_K]
    dest = pad_starts[top_idx] + rank[:, :TOP_K]
    token = jnp.broadcast_to(jnp.arange(t, dtype=jnp.int32)[:, None], (t, TOP_K))
    row_token = jnp.zeros((n_rows,), jnp.int32).at[dest.reshape(-1)].set(token.reshape(-1))
    tiles_used = (pad_ends[-1] // tm).astype(jnp.int32).reshape(1)
    tile_ids = jnp.arange(nt, dtype=jnp.int32)
    tile_expert = jnp.minimum(
        jnp.searchsorted(pad_ends, tile_ids * tm, side='right'), n_exp - 1).astype(jnp.int32)
    tile_src = jnp.minimum(tile_ids, tiles_used[0] - 1)
    tile_expert = tile_expert[tile_src]

    xs = _dispatch(h2, row_token, tiles_used, tm)
    a = _gateup(xs, wgu_bf, bgu, tile_expert, tile_src, tiles_used, tm)
    ys = _down(a, wd_bf, bd, tile_expert, tile_src, tiles_used, tm)
    return ys, dest.astype(jnp.int32)


def kernel(x_prompt, x_sample, c_prompt, c_sample, w_ada, b_ada, g_mix_pre, g_mix_post, g_ffn_pre, g_ffn_post, w_in, w_branch_gate, rpb, sinks, w_proj_a, w_proj_b, w_o, w_router, b_router, w_gate_up, b_gate_up, w_down, b_down):
    b0, s0, d = x_prompt.shape
    b1, s1, _ = x_sample.shape
    assert b0 == 1 and s0 % s1 == 0
    depth = w_ada.shape[0]
    seqs = _Seqs(s0, b1, s1)
    na_w = w_proj_a.shape[1]
    sw_w = w_proj_b.shape[1]
    kv_w = (w_in.shape[2] - 3 * na_w - sw_w) // 2
    na_heads = na_w // NA_HEAD_DIM
    q_col, k_col, v_col = 3 * na_w, 3 * na_w + sw_w, 3 * na_w + sw_w + kv_w
    moe_tm = _tile(seqs.total * TOP_K, 512)

    x = jnp.concatenate([x_prompt.reshape(s0, d), x_sample.reshape(b1 * s1, d)], axis=0)
    c8 = jnp.zeros((SUBLANES, d), F32).at[:1 + b1].set(
        jnp.concatenate([c_prompt, c_sample], axis=0))
    mod = _adaln(c8, w_ada, b_ada)
    cos_t, sin_t = _rope_tables(s0)

    for l in range(depth):
        mod_rows = mod[l].reshape(SUBLANES * 6, 1, d)
        h, proj = _inproj(x, g_mix_pre[l], mod_rows, w_in[l].astype(BF16), seqs, 1, 0)
        bias = _natten_bias(rpb[l])
        oa = jnp.concatenate([
            _natten(proj, bias, 0, 1, s0, na_w, na_heads),
            _natten(proj, bias, s0, b1, s1, na_w, na_heads)], axis=0)
        qr, kr = _rope(proj, cos_t, sin_t, seqs, q_col, sw_w, k_col, kv_w)
        ob = _swa(qr, kr, proj, sinks[l], seqs, v_col, sw_w, kv_w)
        merged = _merge(h, oa, ob, w_branch_gate[l].astype(BF16),
                        w_proj_a[l].astype(BF16), w_proj_b[l].astype(BF16))
        y = _matmul(merged, w_o[l].astype(BF16))
        x1, h2, idx, wt = _postmix(x, y, g_mix_post[l], g_ffn_pre[l], mod_rows,
                                   w_router[l], b_router[l], seqs, 2, 4, 3)
        ys, dest = _moe(h2, idx, wt, w_gate_up[l].astype(BF16), b_gate_up[l],
                        w_down[l].astype(BF16), b_down[l], moe_tm)
        x = _combine(ys, dest, wt, x1, g_ffn_post[l], mod_rows, seqs, 5)

    return (x[:s0].reshape(b0, s0, d), x[s0:].reshape(b1, s1, d))
```

```python
import functools
import math

import jax
import jax.numpy as jnp
from jax import lax
from jax.experimental import pallas as pl
from jax.experimental.pallas import tpu as pltpu

GRID_W = 64
NA_HEAD_DIM = 128
NA_ROWS = 8
NA_COLS = 16
SW_HEAD_DIM = 64
SW_GROUP = 8
SW_WINDOW = 128
SW_BLOCK = 128
ROPE_THETA = 10000.0
TOP_K = 4
SWIGLU_LIMIT = 7.0
SWIGLU_ALPHA = 1.702
NORM_EPS = 1e-6
NEG_INF = -1e30

NATTEN_AHEAD = 2
DMA_UNROLL = 8
CAST_ROWS = 256

LANES = 128
SUBLANES = 8
VMEM_LIMIT_BYTES = 56 * 1024 * 1024

F32 = jnp.float32
BF16 = jnp.bfloat16


def _params(semantics):
    return pltpu.CompilerParams(dimension_semantics=semantics,
                                vmem_limit_bytes=VMEM_LIMIT_BYTES)


def _tile(n, pref):
    t = min(n, pref)
    while n % t:
        t //= 2
    return t


def _dot(a, b):
    return jnp.dot(a, b, preferred_element_type=F32)


def _dot_nt(a, b):
    return lax.dot_general(a, b, (((1,), (1,)), ((), ())), preferred_element_type=F32)


def _pack_pair(lo, hi):
    lo = pltpu.bitcast(lo.astype(BF16).astype(F32), jnp.uint32)
    hi = pltpu.bitcast(hi.astype(BF16).astype(F32), jnp.uint32)
    return (hi & jnp.uint32(0xFFFF0000)) | (lo >> 16)


def _pack_halves(x):
    h = x.shape[1] // 2
    return _pack_pair(x[:, :h], x[:, h:])


def _unpack_halves(w):
    lo = pltpu.bitcast(w << 16, F32)
    hi = pltpu.bitcast(w & jnp.uint32(0xFFFF0000), F32)
    return lo, hi


class _Seqs:
    def __init__(self, s0, nb1, s1):
        self.s0, self.nb1, self.s1 = s0, nb1, s1
        self.total = s0 + nb1 * s1

    def seq_of_tile(self, i, tile):
        n0 = self.s0 // tile
        per = self.s1 // tile
        return jnp.where(i < n0, 0, 1 + (i - n0) // per)

    def pos_tile(self, i, tile):
        n0 = self.s0 // tile
        per = self.s1 // tile
        return jnp.where(i < n0, i, (i - n0) % per)


def _adaln_kernel(c_ref, w_ref, b_ref, o_ref):
    c = c_ref[...]
    a = (c * jax.nn.sigmoid(c)).astype(BF16)
    o_ref[0] = _dot(a, w_ref[0].astype(BF16)) + b_ref[0]


def _adaln(c8, w_ada, b_ada):
    depth, d, n = w_ada.shape
    tn = _tile(n, 512)
    return pl.pallas_call(
        _adaln_kernel,
        grid=(depth, n // tn),
        in_specs=[
            pl.BlockSpec((SUBLANES, d), lambda l, j: (0, 0)),
            pl.BlockSpec((1, d, tn), lambda l, j: (l, 0, j)),
            pl.BlockSpec((1, 1, tn), lambda l, j: (l, 0, j)),
        ],
        out_specs=pl.BlockSpec((1, SUBLANES, tn), lambda l, j: (l, 0, j)),
        out_shape=jax.ShapeDtypeStruct((depth, SUBLANES, n), F32),
        compiler_params=_params(("arbitrary", "arbitrary")),
        name="adaln",
    )(c8, w_ada, b_ada.reshape(depth, 1, n))


def _rope_table_kernel(inv_ref, cos_ref, sin_ref):
    tm = cos_ref.shape[0]
    base = pl.program_id(0) * tm
    pos = (base + lax.broadcasted_iota(jnp.int32, (tm, LANES), 0)).astype(F32)
    ang = pos * inv_ref[...]
    lane = lax.broadcasted_iota(jnp.int32, (tm, LANES), 1)
    sign = jnp.where(lane % SW_HEAD_DIM < SW_HEAD_DIM // 2, -1.0, 1.0)
    cos_ref[...] = jnp.cos(ang)
    sin_ref[...] = jnp.sin(ang) * sign


def _rope_tables(s_max):
    half = SW_HEAD_DIM // 2
    inv_freq = ROPE_THETA ** (-jnp.arange(0, SW_HEAD_DIM, 2, dtype=F32) / SW_HEAD_DIM)
    inv_lane = jnp.tile(inv_freq, LANES // half).reshape(1, LANES)
    tm = _tile(s_max, 512)
    return pl.pallas_call(
        _rope_table_kernel,
        grid=(s_max // tm,),
        in_specs=[pl.BlockSpec((1, LANES), lambda i: (0, 0))],
        out_specs=[pl.BlockSpec((tm, LANES), lambda i: (i, 0))] * 2,
        out_shape=[jax.ShapeDtypeStruct((s_max, LANES), F32)] * 2,
        compiler_params=_params(("arbitrary",)),
        name="rope_tables",
    )(inv_lane)


def _rpb_expand_kernel(r_ref, s_ref, m_ref, o_ref):
    o_ref[...] = jnp.dot(r_ref[...], s_ref[...], preferred_element_type=F32,
                         precision=lax.Precision.HIGHEST) + m_ref[...]


def _natten_bias(rpb_l):
    heads = rpb_l.shape[0]
    nr, nc = 2 * NA_ROWS - 1, 2 * NA_COLS - 1
    nc_pad = 32
    col = jnp.arange(GRID_W)
    col_start = jnp.clip(col - NA_COLS // 2, 0, GRID_W - NA_COLS)
    cc = jnp.arange(GRID_W)[None, :]
    in_win = (cc >= col_start[:, None]) & (cc < col_start[:, None] + NA_COLS)
    dc = cc - col[:, None] + (NA_COLS - 1)
    shift = ((dc[None] == jnp.arange(nc_pad)[:, None, None]) & in_win[None]).astype(F32)
    shift = shift.reshape(nc_pad, GRID_W * GRID_W)
    negmask = jnp.where(in_win, 0.0, NEG_INF).astype(F32).reshape(1, GRID_W * GRID_W)
    rows = heads * nr
    rows_pad = -(-rows // SUBLANES) * SUBLANES
    r2 = jnp.zeros((rows_pad, nc_pad), F32).at[:rows, :nc].set(rpb_l.reshape(rows, nc).astype(F32))
    tiles = pl.pallas_call(
        _rpb_expand_kernel,
        out_shape=jax.ShapeDtypeStruct((rows_pad, GRID_W * GRID_W), F32),
        name="rpb_expand",
    )(r2, shift, negmask)
    tiles = tiles[:rows].reshape(heads, nr, GRID_W, GRID_W)
    sel = jnp.arange(NA_ROWS)[:, None] + jnp.arange(NA_ROWS)[None, :]
    var = tiles[:, sel]
    var = jnp.transpose(var, (0, 1, 3, 2, 4))
    return var.reshape(heads, NA_ROWS, GRID_W, NA_ROWS * GRID_W)


def _norm_mod(x, g, sc, sh):
    ms = jnp.mean(x * x, axis=-1, keepdims=True)
    y = x * lax.rsqrt(ms + NORM_EPS) * g
    return y * (1.0 + sc) + sh


def _inproj_kernel(x_ref, g_ref, sc_ref, sh_ref, w_ref, h_ref, o_ref, h_sc):
    @pl.when(pl.program_id(1) == 0)
    def _():
        h = _norm_mod(x_ref[...], g_ref[...], sc_ref[0], sh_ref[0]).astype(BF16)
        h_sc[...] = h
        h_ref[...] = h

    o_ref[...] = _dot(h_sc[...], w_ref[0]).astype(BF16)


def _inproj(x, g, mod_rows, w_bf, layer, seqs, sc_idx, sh_idx):
    t, d = x.shape
    n = w_bf.shape[2]
    tm = _tile(seqs.s1, 512)
    tn = _tile(n, 512)
    mod_spec = lambda which: pl.BlockSpec(
        (1, 1, d), lambda i, j: (seqs.seq_of_tile(i, tm) * 6 + which, 0, 0))
    return pl.pallas_call(
        _inproj_kernel,
        grid=(t // tm, n // tn),
        in_specs=[
            pl.BlockSpec((tm, d), lambda i, j: (i, 0)),
            pl.BlockSpec((1, d), lambda i, j: (0, 0)),
            mod_spec(sc_idx),
            mod_spec(sh_idx),
            pl.BlockSpec((1, d, tn), lambda i, j: (layer, 0, j)),
        ],
        out_specs=[
            pl.BlockSpec((tm, d), lambda i, j: (i, 0)),
            pl.BlockSpec((tm, tn), lambda i, j: (i, j)),
        ],
        out_shape=[jax.ShapeDtypeStruct((t, d), BF16), jax.ShapeDtypeStruct((t, n), BF16)],
        scratch_shapes=[pltpu.VMEM((tm, d), BF16)],
        compiler_params=_params(("arbitrary", "arbitrary")),
        name="inproj",
    )(x, g.reshape(1, d), mod_rows, mod_rows, w_bf)


def _rope_kernel(q_ref, k_ref, cos_ref, sin_ref, qo_ref, ko_ref, *, q_scale):
    cos = cos_ref[...]
    sin = sin_ref[...]
    lane = lax.broadcasted_iota(jnp.int32, cos.shape, 1)
    first_half = lane % SW_HEAD_DIM < SW_HEAD_DIM // 2

    def rot(x):
        partner = jnp.where(first_half,
                            pltpu.roll(x, LANES - SW_HEAD_DIM // 2, 1),
                            pltpu.roll(x, SW_HEAD_DIM // 2, 1))
        return x * cos + partner * sin

    for c in range(q_ref.shape[1] // LANES):
        sl = slice(c * LANES, (c + 1) * LANES)
        qo_ref[:, sl] = (rot(q_ref[:, sl].astype(F32)) * q_scale).astype(BF16)
    for c in range(k_ref.shape[1] // LANES):
        sl = slice(c * LANES, (c + 1) * LANES)
        ko_ref[:, sl] = rot(k_ref[:, sl].astype(F32)).astype(BF16)


def _rope(proj, cos_t, sin_t, seqs, q_col, q_w, k_col, k_w):
    t = proj.shape[0]
    tm = _tile(seqs.s1, 512)
    tab = pl.BlockSpec((tm, LANES), lambda i: (seqs.pos_tile(i, tm), 0))
    return pl.pallas_call(
        functools.partial(_rope_kernel, q_scale=SW_HEAD_DIM ** -0.5),
        grid=(t // tm,),
        in_specs=[
            pl.BlockSpec((tm, q_w), lambda i: (i, q_col // q_w)),
            pl.BlockSpec((tm, k_w), lambda i: (i, k_col // k_w)),
            tab, tab,
        ],
        out_specs=[pl.BlockSpec((tm, q_w), lambda i: (i, 0)),
                   pl.BlockSpec((tm, k_w), lambda i: (i, 0))],
        out_shape=[jax.ShapeDtypeStruct((t, q_w), BF16), jax.ShapeDtypeStruct((t, k_w), BF16)],
        compiler_params=_params(("arbitrary",)),
        name="rope",
    )(proj, proj, cos_t, sin_t)


def _natten_kernel(q_ref, k_ref, v_ref, b_ref, o_ref, *, rows, rows_per_step):
    kr = min(NA_ROWS, rows)
    scale = NA_HEAD_DIM ** -0.5
    base = pl.program_id(2) * rows_per_step

    def scores(rr):
        i = base + rr
        row_start = jnp.clip(i - kr // 2, 0, rows - kr)
        off = row_start - i + (NA_ROWS - 1)
        kstart = pl.multiple_of(row_start * GRID_W, GRID_W)
        q = q_ref[rr * GRID_W:(rr + 1) * GRID_W, :]
        ks = k_ref[pl.ds(kstart, kr * GRID_W), :]
        return _dot_nt(q, ks) * scale + b_ref[0, off], kstart

    def finish(rr, s, kstart):
        vs = v_ref[pl.ds(kstart, kr * GRID_W), :]
        m = jnp.max(s, axis=-1, keepdims=True)
        e = jnp.exp(s - m)
        l = jnp.sum(e, axis=-1, keepdims=True)
        o = _dot(e.astype(BF16), vs) / l
        o_ref[rr * GRID_W:(rr + 1) * GRID_W, :] = o.astype(BF16)

    ahead = min(NATTEN_AHEAD, rows_per_step)
    pending = [scores(rr) for rr in range(ahead)]
    for rr in range(rows_per_step):
        if rr + ahead < rows_per_step:
            pending.append(scores(rr + ahead))
        finish(rr, *pending.pop(0))


def _natten(proj, bias, tok0, n_seq, s, width, heads):
    rows = s // GRID_W
    assert rows >= NA_ROWS
    rps = _tile(rows, 16)
    tq = rps * GRID_W
    nh = width // NA_HEAD_DIM
    qb0 = tok0 // tq
    sb0 = tok0 // s
    return pl.pallas_call(
        functools.partial(_natten_kernel, rows=rows, rows_per_step=rps),
        grid=(n_seq, heads, rows // rps),
        in_specs=[
            pl.BlockSpec((tq, NA_HEAD_DIM), lambda b, h, r: (qb0 + b * (rows // rps) + r, h)),
            pl.BlockSpec((s, NA_HEAD_DIM), lambda b, h, r: (sb0 + b, nh + h)),
            pl.BlockSpec((s, NA_HEAD_DIM), lambda b, h, r: (sb0 + b, 2 * nh + h)),
            pl.BlockSpec((1, NA_ROWS, GRID_W, NA_ROWS * GRID_W), lambda b, h, r: (h, 0, 0, 0)),
        ],
        out_specs=pl.BlockSpec((tq, NA_HEAD_DIM), lambda b, h, r: (b * (rows // rps) + r, h)),
        out_shape=jax.ShapeDtypeStruct((n_seq * s, width), BF16),
        compiler_params=_params(("arbitrary", "arbitrary", "arbitrary")),
        name="natten",
    )(proj, proj, proj, bias)


def _swa_kernel(sink_ref, q_ref, kp_ref, kc_ref, kn_ref, vp_ref, vc_ref, vn_ref, o_ref,
                *, seqs, kv_heads):
    n = pl.program_id(0)
    pos_blk = seqs.pos_tile(n, SW_BLOCK)
    n_blk = jnp.where(n < seqs.s0 // SW_BLOCK, seqs.s0 // SW_BLOCK, seqs.s1 // SW_BLOCK)
    is_first = pos_blk == 0
    is_last = pos_blk == n_blk - 1

    band = 3 * SW_BLOCK
    mpos = lax.broadcasted_iota(jnp.int32, (band, SW_BLOCK), 0)
    qpos = lax.broadcasted_iota(jnp.int32, (band, SW_BLOCK), 1)
    rel = mpos - SW_BLOCK - qpos
    valid = (jnp.abs(rel) <= SW_WINDOW)
    valid = valid & ((mpos >= SW_BLOCK) | jnp.logical_not(is_first))
    valid = valid & ((mpos < 2 * SW_BLOCK) | jnp.logical_not(is_last))

    kb = jnp.concatenate([kp_ref[...], kc_ref[...], kn_ref[...]], axis=0).astype(F32)
    vb = jnp.concatenate([vp_ref[...], vc_ref[...], vn_ref[...]], axis=0).astype(F32)
    vb_t = vb.T.astype(BF16)
    low = lax.broadcasted_iota(jnp.int32, (band, LANES), 1) < SW_HEAD_DIM
    qlane_low = lax.broadcasted_iota(jnp.int32, (SW_BLOCK, LANES), 1) < SW_HEAD_DIM

    for k in range(kv_heads):
        kpair = kb[:, (k // 2) * LANES:(k // 2 + 1) * LANES]
        kroll = pltpu.roll(kpair, SW_HEAD_DIM, 1)
        k2 = (jnp.where(low, kpair, kroll) if k % 2 == 0 else jnp.where(low, kroll, kpair))
        k2 = k2.astype(BF16)
        v_t = vb_t[k * SW_HEAD_DIM:(k + 1) * SW_HEAD_DIM, :]

        qs = []
        for g in range(SW_GROUP):
            hq = k * SW_GROUP + g
            q2 = q_ref[:, (hq // 2) * LANES:(hq // 2 + 1) * LANES]
            keep = qlane_low if hq % 2 == 0 else jnp.logical_not(qlane_low)
            qs.append(jnp.where(keep, q2, jnp.zeros_like(q2)))
        qm = jnp.concatenate(qs, axis=0)
        s_t = _dot_nt(k2, qm)
        for pp in range(SW_GROUP // 2):
            halves = []
            for g in (2 * pp, 2 * pp + 1):
                sink = sink_ref[k * SW_GROUP + g]
                s = jnp.where(valid, s_t[:, g * SW_BLOCK:(g + 1) * SW_BLOCK], NEG_INF)
                m = jnp.maximum(jnp.max(s, axis=0, keepdims=True), sink)
                e = jnp.exp(s - m)
                denom = jnp.sum(e, axis=0, keepdims=True) + jnp.exp(sink - m)
                halves.append(_dot(v_t, e.astype(BF16)) / denom)
            o_t = jnp.concatenate(halves, axis=0)
            c0 = (k * SW_GROUP // 2 + pp) * LANES
            o_ref[:, c0:c0 + LANES] = o_t.T.astype(BF16)


def _swa(qr, kr, proj, sinks_l, seqs, v_col, q_w, kv_w):
    t = qr.shape[0]
    nblk = t // SW_BLOCK
    kv_heads = kv_w // SW_HEAD_DIM
    assert kv_heads % 2 == 0 and kv_w % LANES == 0

    def prev(n, s):
        return jnp.maximum(n - 1, 0)

    def nxt(n, s):
        return jnp.minimum(n + 1, nblk - 1)

    vcb = v_col // kv_w
    grid_spec = pltpu.PrefetchScalarGridSpec(
        num_scalar_prefetch=1,
        grid=(nblk,),
        in_specs=[
            pl.BlockSpec((SW_BLOCK, q_w), lambda n, s: (n, 0)),
            pl.BlockSpec((SW_BLOCK, kv_w), lambda n, s: (prev(n, s), 0)),
            pl.BlockSpec((SW_BLOCK, kv_w), lambda n, s: (n, 0)),
            pl.BlockSpec((SW_BLOCK, kv_w), lambda n, s: (nxt(n, s), 0)),
            pl.BlockSpec((SW_BLOCK, kv_w), lambda n, s: (prev(n, s), vcb)),
            pl.BlockSpec((SW_BLOCK, kv_w), lambda n, s: (n, vcb)),
            pl.BlockSpec((SW_BLOCK, kv_w), lambda n, s: (nxt(n, s), vcb)),
        ],
        out_specs=pl.BlockSpec((SW_BLOCK, q_w), lambda n, s: (n, 0)),
    )
    return pl.pallas_call(
        functools.partial(_swa_kernel, seqs=seqs, kv_heads=kv_heads),
        grid_spec=grid_spec,
        out_shape=jax.ShapeDtypeStruct((t, q_w), BF16),
        compiler_params=_params(("arbitrary",)),
        name="swa",
    )(sinks_l.astype(F32), qr, kr, kr, kr, proj, proj, proj)


def _merge_kernel(h_ref, oa_ref, ob_ref, wga_ref, wgb_ref, wa_ref, wb_ref, o_ref):
    h = h_ref[...]
    ga = jax.nn.sigmoid(_dot(h, wga_ref[0]))
    gb = jax.nn.sigmoid(_dot(h, wgb_ref[0]))
    m = ga * _dot(oa_ref[...], wa_ref[0]) + gb * _dot(ob_ref[...], wb_ref[0])
    o_ref[...] = m.astype(BF16)


def _merge(h, oa, ob, wg_bf, wa_bf, wb_bf, layer):
    t, d = h.shape
    wa_w, wb_w = oa.shape[1], ob.shape[1]
    tm = _tile(t, 512)
    tn = _tile(d, 512)
    nj = d // tn
    return pl.pallas_call(
        _merge_kernel,
        grid=(t // tm, nj),
        in_specs=[
            pl.BlockSpec((tm, d), lambda i, j: (i, 0)),
            pl.BlockSpec((tm, wa_w), lambda i, j: (i, 0)),
            pl.BlockSpec((tm, wb_w), lambda i, j: (i, 0)),
            pl.BlockSpec((1, d, tn), lambda i, j: (layer, 0, j)),
            pl.BlockSpec((1, d, tn), lambda i, j: (layer, 0, nj + j)),
            pl.BlockSpec((1, wa_w, tn), lambda i, j: (layer, 0, j)),
            pl.BlockSpec((1, wb_w, tn), lambda i, j: (layer, 0, j)),
        ],
        out_specs=pl.BlockSpec((tm, tn), lambda i, j: (i, j)),
        out_shape=jax.ShapeDtypeStruct((t, d), BF16),
        compiler_params=_params(("arbitrary", "arbitrary")),
        name="merge",
    )(h, oa, ob, wg_bf, wg_bf, wa_bf, wb_bf)


def _matmul_kernel(x_ref, w_ref, o_ref):
    o_ref[...] = _dot(x_ref[...], w_ref[0]).astype(o_ref.dtype)


def _matmul(x, w, layer, tm_pref=512, tn_pref=1024):
    t, k = x.shape
    n = w.shape[2]
    tm, tn = _tile(t, tm_pref), _tile(n, tn_pref)
    return pl.pallas_call(
        _matmul_kernel,
        grid=(t // tm, n // tn),
        in_specs=[pl.BlockSpec((tm, k), lambda i, j: (i, 0)),
                  pl.BlockSpec((1, k, tn), lambda i, j: (layer, 0, j))],
        out_specs=pl.BlockSpec((tm, tn), lambda i, j: (i, j)),
        out_shape=jax.ShapeDtypeStruct((t, n), BF16),
        compiler_params=_params(("arbitrary", "arbitrary")),
        name="matmul",
    )(x, w)


def _rms(x, g):
    ms = jnp.mean(x * x, axis=-1, keepdims=True)
    return x * lax.rsqrt(ms + NORM_EPS) * g


def _postmix_kernel(x_ref, y_ref, gp_ref, gt_ref, gf_ref, sc_ref, sh_ref, wr_ref, br_ref,
                    x1_ref, h2_ref, idx_ref, wt_ref, *, n_experts):
    x1 = x_ref[...] + gt_ref[0] * _rms(y_ref[...].astype(F32), gp_ref[...])
    x1_ref[...] = x1
    h2 = _rms(x1, gf_ref[...]) * (1.0 + sc_ref[0]) + sh_ref[0]
    h2_ref[...] = _pack_halves(h2)
    logits = jnp.dot(h2, wr_ref[...], preferred_element_type=F32,
                     precision=lax.Precision.HIGHEST) + br_ref[...]
    lane = lax.broadcasted_iota(jnp.int32, logits.shape, 1)
    work = jnp.where(lane < n_experts, logits, -jnp.inf)
    idx_out = jnp.zeros(logits.shape, jnp.int32)
    val_out = jnp.zeros(logits.shape, F32)
    vals = []
    for k in range(TOP_K):
        m = jnp.max(work, axis=-1, keepdims=True)
        sel = jnp.min(jnp.where(work == m, lane, LANES), axis=-1, keepdims=True)
        idx_out = jnp.where(lane == k, sel, idx_out)
        vals.append(m)
        work = jnp.where(lane == sel, -jnp.inf, work)
    es = [jnp.exp(v - vals[0]) for v in vals]
    tot = es[0]
    for e in es[1:]:
        tot = tot + e
    for k in range(TOP_K):
        val_out = jnp.where(lane == k, es[k] / tot, val_out)
    idx_ref[...] = idx_out
    wt_ref[...] = val_out


def _postmix(x, y, g_post, g_ffn, mod_rows, w_router, b_router, seqs, gt_idx, sc_idx, sh_idx):
    t, d = x.shape
    e = w_router.shape[1]
    tm = _tile(seqs.s1, 256)
    wr = jnp.zeros((d, LANES), F32).at[:, :e].set(w_router)
    br = jnp.zeros((1, LANES), F32).at[0, :e].set(b_router)
    mod_spec = lambda which: pl.BlockSpec(
        (1, 1, d), lambda i: (seqs.seq_of_tile(i, tm) * 6 + which, 0, 0))
    row = pl.BlockSpec((tm, d), lambda i: (i, 0))
    vec = pl.BlockSpec((1, d), lambda i: (0, 0))
    small = pl.BlockSpec((tm, LANES), lambda i: (i, 0))
    return pl.pallas_call(
        functools.partial(_postmix_kernel, n_experts=e),
        grid=(t // tm,),
        in_specs=[row, row, vec, mod_spec(gt_idx), vec, mod_spec(sc_idx), mod_spec(sh_idx),
                  pl.BlockSpec((d, LANES), lambda i: (0, 0)),
                  pl.BlockSpec((1, LANES), lambda i: (0, 0))],
        out_specs=[row, pl.BlockSpec((tm, d // 2), lambda i: (i, 0)), small, small],
        out_shape=[jax.ShapeDtypeStruct((t, d), F32), jax.ShapeDtypeStruct((t, d // 2), jnp.uint32),
                   jax.ShapeDtypeStruct((t, LANES), jnp.int32),
                   jax.ShapeDtypeStruct((t, LANES), F32)],
        compiler_params=_params(("arbitrary",)),
        name="postmix",
    )(x, y, g_post.reshape(1, d), mod_rows, g_ffn.reshape(1, d), mod_rows, mod_rows, wr, br)


def _rank_kernel(idx_ref, rank_ref, cnt_ref, carry):
    @pl.when(pl.program_id(0) == 0)
    def _():
        carry[...] = jnp.zeros_like(carry)

    idx = idx_ref[...]
    tb = idx.shape[0]
    lane = lax.broadcasted_iota(jnp.int32, idx.shape, 1)
    hits = [lane == idx[:, k:k + 1] for k in range(TOP_K)]
    onehot = jnp.zeros(idx.shape, F32)
    for hit in hits:
        onehot = onehot + hit.astype(F32)
    r = lax.broadcasted_iota(jnp.int32, (tb, tb), 0)
    c = lax.broadcasted_iota(jnp.int32, (tb, tb), 1)
    tri = (c < r).astype(BF16)
    before = _dot(tri, onehot.astype(BF16)) + carry[...]
    rank = jnp.zeros(idx.shape, F32)
    for k, hit in enumerate(hits):
        rk = jnp.sum(jnp.where(hit, before, 0.0), axis=-1, keepdims=True)
        rank = jnp.where(lane == k, rk, rank)
    rank_ref[...] = rank.astype(jnp.int32)
    carry[...] = carry[...] + jnp.sum(onehot, axis=0, keepdims=True)
    cnt_ref[...] = jnp.broadcast_to(carry[...], cnt_ref.shape).astype(jnp.int32)


def _ranks(idx):
    t = idx.shape[0]
    tb = _tile(t, 256)
    return pl.pallas_call(
        _rank_kernel,
        grid=(t // tb,),
        in_specs=[pl.BlockSpec((tb, LANES), lambda i: (i, 0))],
        out_specs=[pl.BlockSpec((tb, LANES), lambda i: (i, 0)),
                   pl.BlockSpec((SUBLANES, LANES), lambda i: (0, 0))],
        out_shape=[jax.ShapeDtypeStruct((t, LANES), jnp.int32),
                   jax.ShapeDtypeStruct((SUBLANES, LANES), jnp.int32)],
        scratch_shapes=[pltpu.VMEM((1, LANES), F32)],
        compiler_params=_params(("arbitrary",)),
        name="ranks",
    )(idx)


def _row_copy(src_hbm, row, dst, slot, sem):
    return pltpu.make_async_copy(src_hbm.at[pl.ds(row, 1), :], dst.at[pl.ds(slot, 1), :], sem)


def _start_rows(src_hbm, row_of, dst, sem, n):
    def body(c, carry):
        for u in range(DMA_UNROLL):
            i = c * DMA_UNROLL + u
            _row_copy(src_hbm, row_of(i), dst, i, sem).start(priority=u % 2)
        return carry

    lax.fori_loop(0, n // DMA_UNROLL, body, 0)


def _wait_rows(src_hbm, dst, sem, n):
    def body(i, carry):
        _row_copy(src_hbm, 0, dst, i, sem).wait()
        return carry

    lax.fori_loop(0, n, body, 0)


def _dispatch_kernel(tok_ref, nxt_ref, used_ref, h_hbm, o_ref, buf, sem):
    r = pl.program_id(0)
    used = used_ref[0]
    tg = buf.shape[1]
    half = buf.shape[2]

    @pl.when(r == 0)
    def _():
        _start_rows(h_hbm, lambda i: tok_ref[0, 0, i], buf.at[0], sem.at[0], tg)

    @pl.when(r + 1 < used)
    def _():
        slot = (r + 1) % 2
        _start_rows(h_hbm, lambda i: nxt_ref[0, 0, i], buf.at[slot], sem.at[slot], tg)

    @pl.when(r < used)
    def _():
        slot = r % 2
        _wait_rows(h_hbm, buf.at[slot], sem.at[slot], tg)
        lo, hi = _unpack_halves(buf[slot])
        o_ref[:, :half] = lo.astype(BF16)
        o_ref[:, half:] = hi.astype(BF16)

    @pl.when(r >= used)
    def _():
        o_ref[...] = jnp.zeros_like(o_ref)


def _dispatch(h2p, row_token, tiles_used, tg):
    n_rows = row_token.shape[0]
    half = h2p.shape[1]
    nt = n_rows // tg
    assert tg % DMA_UNROLL == 0
    tok = row_token.reshape(nt, 1, tg)
    grid_spec = pltpu.PrefetchScalarGridSpec(
        num_scalar_prefetch=0,
        grid=(nt,),
        in_specs=[
            pl.BlockSpec((1, 1, tg), lambda r: (r, 0, 0), memory_space=pltpu.SMEM),
            pl.BlockSpec((1, 1, tg), lambda r: (jnp.minimum(r + 1, nt - 1), 0, 0),
                         memory_space=pltpu.SMEM),
            pl.BlockSpec(memory_space=pltpu.SMEM),
            pl.BlockSpec(memory_space=pl.ANY),
        ],
        out_specs=pl.BlockSpec((tg, 2 * half), lambda r: (r, 0)),
        scratch_shapes=[pltpu.VMEM((2, tg, half), jnp.uint32), pltpu.SemaphoreType.DMA((2,))],
    )
    return pl.pallas_call(
        _dispatch_kernel,
        grid_spec=grid_spec,
        out_shape=jax.ShapeDtypeStruct((n_rows, 2 * half), BF16),
        compiler_params=_params(("arbitrary",)),
        name="dispatch",
    )(tok, tok, tiles_used, h2p)


def _cast_weights_on_expert_change(e_ref, r, used, w_refs, w_sc):
    changed = (r == 0) | (e_ref[r] != e_ref[jnp.maximum(r - 1, 0)])

    @pl.when(changed & (r < used))
    def _():
        k = w_refs[0].shape[2]
        rows = min(CAST_ROWS, k)

        def body(c, carry):
            sl = pl.ds(pl.multiple_of(c * rows, rows), rows)
            for i, w_ref in enumerate(w_refs):
                w_sc[i, sl, :] = w_ref[0, 0, sl, :].astype(BF16)
            return carry

        lax.fori_loop(0, k // rows, body, 0)


def _gateup_kernel(e_ref, src_ref, used_ref, x_ref, wg_ref, wu_ref, bg_ref, bu_ref, o_ref, w_sc):
    r = pl.program_id(1)
    used = used_ref[0]
    _cast_weights_on_expert_change(e_ref, r, used, (wg_ref, wu_ref), w_sc)

    @pl.when(r < used)
    def _():
        x = x_ref[...]
        gate = _dot(x, w_sc[0]) + bg_ref[0, 0]
        up = _dot(x, w_sc[1]) + bu_ref[0, 0]
        gate = jnp.minimum(gate, SWIGLU_LIMIT)
        up = jnp.clip(up, -SWIGLU_LIMIT, SWIGLU_LIMIT)
        glu = gate * jax.nn.sigmoid(gate * SWIGLU_ALPHA)
        o_ref[...] = ((up + 1.0) * glu).astype(BF16)

    @pl.when(r >= used)
    def _():
        o_ref[...] = jnp.zeros_like(o_ref)


def _gateup(xs, w_gate_up, b_gate_up, layer, tile_expert, tile_src, tiles_used, tm):
    n_rows, d = xs.shape
    _, n_exp, _, two_f = w_gate_up.shape
    f = two_f // 2
    tn = _tile(f, 512)
    nj = f // tn
    nt = n_rows // tm
    b4 = b_gate_up.reshape(-1, n_exp, 1, two_f)
    grid_spec = pltpu.PrefetchScalarGridSpec(
        num_scalar_prefetch=3,
        grid=(nj, nt),
        in_specs=[
            pl.BlockSpec((tm, d), lambda j, r, e, s, u: (s[r], 0)),
            pl.BlockSpec((1, 1, d, tn), lambda j, r, e, s, u: (layer, e[r], 0, j)),
            pl.BlockSpec((1, 1, d, tn), lambda j, r, e, s, u: (layer, e[r], 0, nj + j)),
            pl.BlockSpec((1, 1, 1, tn), lambda j, r, e, s, u: (layer, e[r], 0, j)),
            pl.BlockSpec((1, 1, 1, tn), lambda j, r, e, s, u: (layer, e[r], 0, nj + j)),
        ],
        out_specs=pl.BlockSpec((tm, tn), lambda j, r, e, s, u: (r, j)),
        scratch_shapes=[pltpu.VMEM((2, d, tn), BF16)],
    )
    return pl.pallas_call(
        _gateup_kernel,
        grid_spec=grid_spec,
        out_shape=jax.ShapeDtypeStruct((n_rows, f), BF16),
        compiler_params=_params(("arbitrary", "arbitrary")),
        name="gateup",
    )(tile_expert, tile_src, tiles_used, xs, w_gate_up, w_gate_up, b4, b4)


def _down_kernel(e_ref, src_ref, used_ref, a_ref, wl_ref, wh_ref, bl_ref, bh_ref, o_ref, w_sc):
    r = pl.program_id(1)
    used = used_ref[0]
    _cast_weights_on_expert_change(e_ref, r, used, (wl_ref, wh_ref), w_sc)

    @pl.when(r < used)
    def _():
        a = a_ref[...]
        lo = _dot(a, w_sc[0]) + bl_ref[0, 0]
        hi = _dot(a, w_sc[1]) + bh_ref[0, 0]
        o_ref[...] = _pack_pair(lo, hi)

    @pl.when(r >= used)
    def _():
        o_ref[...] = jnp.zeros_like(o_ref)


def _down(a, w_down, b_down, layer, tile_expert, tile_src, tiles_used, tm):
    n_rows, f = a.shape
    _, n_exp, _, d = w_down.shape
    half = d // 2
    tn = _tile(half, 512)
    nj = half // tn
    nt = n_rows // tm
    b4 = b_down.reshape(-1, n_exp, 1, d)
    grid_spec = pltpu.PrefetchScalarGridSpec(
        num_scalar_prefetch=3,
        grid=(nj, nt),
        in_specs=[
            pl.BlockSpec((tm, f), lambda j, r, e, s, u: (s[r], 0)),
            pl.BlockSpec((1, 1, f, tn), lambda j, r, e, s, u: (layer, e[r], 0, j)),
            pl.BlockSpec((1, 1, f, tn), lambda j, r, e, s, u: (layer, e[r], 0, nj + j)),
            pl.BlockSpec((1, 1, 1, tn), lambda j, r, e, s, u: (layer, e[r], 0, j)),
            pl.BlockSpec((1, 1, 1, tn), lambda j, r, e, s, u: (layer, e[r], 0, nj + j)),
        ],
        out_specs=pl.BlockSpec((tm, tn), lambda j, r, e, s, u: (r, j)),
        scratch_shapes=[pltpu.VMEM((2, f, tn), BF16)],
    )
    return pl.pallas_call(
        _down_kernel,
        grid_spec=grid_spec,
        out_shape=jax.ShapeDtypeStruct((n_rows, half), jnp.uint32),
        compiler_params=_params(("arbitrary", "arbitrary")),
        name="down",
    )(tile_expert, tile_src, tiles_used, a, w_down, w_down, b4, b4)


def _combine_kernel(dest_ref, nxt_ref, ys_hbm, wt_ref, x_ref, g_ref, gt_ref, o_ref, buf, sem,
                    *, n):
    i = pl.program_id(0)
    tc = x_ref.shape[0]
    half = buf.shape[3]

    def start(idx_ref, slot):
        for k in range(TOP_K):
            _start_rows(ys_hbm, lambda t, k=k: idx_ref[0, k, t], buf.at[slot, k], sem.at[slot], tc)

    @pl.when(i == 0)
    def _():
        start(dest_ref, 0)

    @pl.when(i + 1 < n)
    def _():
        start(nxt_ref, (i + 1) % 2)

    slot = i % 2
    for k in range(TOP_K):
        _wait_rows(ys_hbm, buf.at[slot, k], sem.at[slot], tc)
    wt = wt_ref[...]
    y_lo = jnp.zeros((tc, half), F32)
    y_hi = jnp.zeros((tc, half), F32)
    for k in range(TOP_K):
        lo, hi = _unpack_halves(buf[slot, k])
        y_lo = y_lo + wt[:, k:k + 1] * lo
        y_hi = y_hi + wt[:, k:k + 1] * hi
    y = jnp.concatenate([y_lo, y_hi], axis=1)
    o_ref[...] = x_ref[...] + gt_ref[0] * _rms(y, g_ref[...])


def _combine(ysp, dest, wt, x1, g_post, mod_rows, seqs, gt_idx, tok0, n_tok):
    t, d = x1.shape
    tc = _tile(seqs.s1, 128)
    assert tc % DMA_UNROLL == 0 and tok0 % tc == 0 and n_tok % tc == 0
    nt = n_tok // tc
    t0 = tok0 // tc
    dest_t = dest.reshape(t // tc, tc, TOP_K).transpose(0, 2, 1)
    grid_spec = pltpu.PrefetchScalarGridSpec(
        num_scalar_prefetch=0,
        grid=(nt,),
        in_specs=[
            pl.BlockSpec((1, TOP_K, tc), lambda i: (t0 + i, 0, 0), memory_space=pltpu.SMEM),
            pl.BlockSpec((1, TOP_K, tc), lambda i: (t0 + jnp.minimum(i + 1, nt - 1), 0, 0),
                         memory_space=pltpu.SMEM),
            pl.BlockSpec(memory_space=pl.ANY),
            pl.BlockSpec((tc, LANES), lambda i: (t0 + i, 0)),
            pl.BlockSpec((tc, d), lambda i: (t0 + i, 0)),
            pl.BlockSpec((1, d), lambda i: (0, 0)),
            pl.BlockSpec((1, 1, d),
                         lambda i: (seqs.seq_of_tile(t0 + i, tc) * 6 + gt_idx, 0, 0)),
        ],
        out_specs=pl.BlockSpec((tc, d), lambda i: (i, 0)),
        scratch_shapes=[pltpu.VMEM((2, TOP_K, tc, d // 2), jnp.uint32),
                        pltpu.SemaphoreType.DMA((2,))],
    )
    return pl.pallas_call(
        functools.partial(_combine_kernel, n=nt),
        grid_spec=grid_spec,
        out_shape=jax.ShapeDtypeStruct((n_tok, d), F32),
        compiler_params=_params(("arbitrary",)),
        name="combine",
    )(dest_t, dest_t, ysp, wt, x1, g_post.reshape(1, d), mod_rows)


def _moe(h2p, idx, w_gate_up, b_gate_up, w_down, b_down, layer, tm):
    t = h2p.shape[0]
    n_exp = w_gate_up.shape[1]
    rank, cnt = _ranks(idx)
    counts = cnt[0, :n_exp]
    padded = (counts + tm - 1) // tm * tm
    pad_ends = jnp.cumsum(padded)
    pad_starts = pad_ends - padded
    n_assign = t * TOP_K
    nt = -(-n_assign // tm) + n_exp
    n_rows = nt * tm
    top_idx = idx[:, :TOP_K]
    dest = pad_starts[top_idx] + rank[:, :TOP_K]
    token = jnp.broadcast_to(jnp.arange(t, dtype=jnp.int32)[:, None], (t, TOP_K))
    row_token = jnp.zeros((n_rows,), jnp.int32).at[dest.reshape(-1)].set(token.reshape(-1))
    tiles_used = (pad_ends[-1] // tm).astype(jnp.int32).reshape(1)
    tile_ids = jnp.arange(nt, dtype=jnp.int32)
    tile_expert = jnp.minimum(
        jnp.searchsorted(pad_ends, tile_ids * tm, side='right'), n_exp - 1).astype(jnp.int32)
    tile_src = jnp.minimum(tile_ids, tiles_used[0] - 1)
    tile_expert = tile_expert[tile_src]

    xs = _dispatch(h2p, row_token, tiles_used, tm)
    a = _gateup(xs, w_gate_up, b_gate_up, layer, tile_expert, tile_src, tiles_used, tm)
    ysp = _down(a, w_down, b_down, layer, tile_expert, tile_src, tiles_used, tm)
    return ysp, dest.astype(jnp.int32)


def kernel(x_prompt, x_sample, c_prompt, c_sample, w_ada, b_ada, g_mix_pre, g_mix_post, g_ffn_pre, g_ffn_post, w_in, w_branch_gate, rpb, sinks, w_proj_a, w_proj_b, w_o, w_router, b_router, w_gate_up, b_gate_up, w_down, b_down):
    b0, s0, d = x_prompt.shape
    b1, s1, _ = x_sample.shape
    assert b0 == 1 and s0 % s1 == 0
    depth = w_ada.shape[0]
    seqs = _Seqs(s0, b1, s1)
    na_w = w_proj_a.shape[1]
    sw_w = w_proj_b.shape[1]
    kv_w = (w_in.shape[2] - 3 * na_w - sw_w) // 2
    na_heads = na_w // NA_HEAD_DIM
    q_col, k_col, v_col = 3 * na_w, 3 * na_w + sw_w, 3 * na_w + sw_w + kv_w
    moe_tm = _tile(seqs.total * TOP_K, 512)

    x = jnp.concatenate([x_prompt.reshape(s0, d), x_sample.reshape(b1 * s1, d)], axis=0)
    c8 = jnp.zeros((SUBLANES, d), F32).at[:1 + b1].set(
        jnp.concatenate([c_prompt, c_sample], axis=0))
    mod = _adaln(c8, w_ada, b_ada)
    cos_t, sin_t = _rope_tables(s0)
    w_in_bf = w_in.astype(BF16)
    wg_bf = w_branch_gate.astype(BF16)
    wa_bf = w_proj_a.astype(BF16)
    wb_bf = w_proj_b.astype(BF16)
    wo_bf = w_o.astype(BF16)

    for l in range(depth):
        mod_rows = mod[l].reshape(SUBLANES * 6, 1, d)
        h, proj = _inproj(x, g_mix_pre[l], mod_rows, w_in_bf, l, seqs, 1, 0)
        bias = _natten_bias(rpb[l])
        oa = jnp.concatenate([
            _natten(proj, bias, 0, 1, s0, na_w, na_heads),
            _natten(proj, bias, s0, b1, s1, na_w, na_heads)], axis=0)
        qr, kr = _rope(proj, cos_t, sin_t, seqs, q_col, sw_w, k_col, kv_w)
        ob = _swa(qr, kr, proj, sinks[l], seqs, v_col, sw_w, kv_w)
        merged = _merge(h, oa, ob, wg_bf, wa_bf, wb_bf, l)
        y = _matmul(merged, wo_bf, l)
        x1, h2p, idx, wt = _postmix(x, y, g_mix_post[l], g_ffn_pre[l], mod_rows,
                                    w_router[l], b_router[l], seqs, 2, 4, 3)
        ysp, dest = _moe(h2p, idx, w_gate_up, b_gate_up, w_down, b_down, l, moe_tm)
        combine = functools.partial(_combine, ysp, dest, wt, x1, g_ffn_post[l], mod_rows, seqs, 5)
        if l + 1 < depth:
            x = combine(0, seqs.total)
        else:
            y_prompt = combine(0, s0)
            y_sample = combine(s0, b1 * s1)

    return (y_prompt.reshape(b0, s0, d), y_sample.reshape(b1, s1, d))
```

```python
import functools
import math

import jax
import jax.numpy as jnp
from jax import lax
from jax.experimental import pallas as pl
from jax.experimental.pallas import tpu as pltpu

GRID_W = 64
NA_HEAD_DIM = 128
NA_ROWS = 8
NA_COLS = 16
SW_HEAD_DIM = 64
SW_GROUP = 8
SW_WINDOW = 128
SW_BLOCK = 128
ROPE_THETA = 10000.0
TOP_K = 4
SWIGLU_LIMIT = 7.0
SWIGLU_ALPHA = 1.702
NORM_EPS = 1e-6
NEG_INF = -1e30

NATTEN_AHEAD = 2
DMA_UNROLL = 8
CAST_ROWS = 256

LANES = 128
SUBLANES = 8
VMEM_LIMIT_BYTES = 56 * 1024 * 1024

F32 = jnp.float32
BF16 = jnp.bfloat16


def _params(semantics):
    return pltpu.CompilerParams(dimension_semantics=semantics,
                                vmem_limit_bytes=VMEM_LIMIT_BYTES)


def _tile(n, pref):
    t = min(n, pref)
    while n % t:
        t //= 2
    return t


def _dot(a, b):
    return jnp.dot(a, b, preferred_element_type=F32)


def _dot_nt(a, b):
    return lax.dot_general(a, b, (((1,), (1,)), ((), ())), preferred_element_type=F32)


def _pack_pair(lo, hi):
    lo = pltpu.bitcast(lo.astype(BF16).astype(F32), jnp.uint32)
    hi = pltpu.bitcast(hi.astype(BF16).astype(F32), jnp.uint32)
    return (hi & jnp.uint32(0xFFFF0000)) | (lo >> 16)


def _pack_halves(x):
    h = x.shape[1] // 2
    return _pack_pair(x[:, :h], x[:, h:])


def _unpack_halves(w):
    lo = pltpu.bitcast(w << 16, F32)
    hi = pltpu.bitcast(w & jnp.uint32(0xFFFF0000), F32)
    return lo, hi


class _Seqs:
    def __init__(self, s0, nb1, s1):
        self.s0, self.nb1, self.s1 = s0, nb1, s1
        self.total = s0 + nb1 * s1

    def seq_of_tile(self, i, tile):
        n0 = self.s0 // tile
        per = self.s1 // tile
        return jnp.where(i < n0, 0, 1 + (i - n0) // per)

    def pos_tile(self, i, tile):
        n0 = self.s0 // tile
        per = self.s1 // tile
        return jnp.where(i < n0, i, (i - n0) % per)


def _adaln_kernel(c_ref, w_ref, b_ref, o_ref):
    c = c_ref[...]
    a = (c * jax.nn.sigmoid(c)).astype(BF16)
    o_ref[0] = _dot(a, w_ref[0].astype(BF16)) + b_ref[0]


def _adaln(c8, w_ada, b_ada):
    depth, d, n = w_ada.shape
    tn = _tile(n, 512)
    return pl.pallas_call(
        _adaln_kernel,
        grid=(depth, n // tn),
        in_specs=[
            pl.BlockSpec((SUBLANES, d), lambda l, j: (0, 0)),
            pl.BlockSpec((1, d, tn), lambda l, j: (l, 0, j)),
            pl.BlockSpec((1, 1, tn), lambda l, j: (l, 0, j)),
        ],
        out_specs=pl.BlockSpec((1, SUBLANES, tn), lambda l, j: (l, 0, j)),
        out_shape=jax.ShapeDtypeStruct((depth, SUBLANES, n), F32),
        compiler_params=_params(("arbitrary", "arbitrary")),
        name="adaln",
    )(c8, w_ada, b_ada.reshape(depth, 1, n))


def _rope_table_kernel(inv_ref, cos_ref, sin_ref):
    tm = cos_ref.shape[0]
    base = pl.program_id(0) * tm
    pos = (base + lax.broadcasted_iota(jnp.int32, (tm, LANES), 0)).astype(F32)
    ang = pos * inv_ref[...]
    lane = lax.broadcasted_iota(jnp.int32, (tm, LANES), 1)
    sign = jnp.where(lane % SW_HEAD_DIM < SW_HEAD_DIM // 2, -1.0, 1.0)
    cos_ref[...] = jnp.cos(ang)
    sin_ref[...] = jnp.sin(ang) * sign


def _rope_tables(s_max):
    half = SW_HEAD_DIM // 2
    inv_freq = ROPE_THETA ** (-jnp.arange(0, SW_HEAD_DIM, 2, dtype=F32) / SW_HEAD_DIM)
    inv_lane = jnp.tile(inv_freq, LANES // half).reshape(1, LANES)
    tm = _tile(s_max, 512)
    return pl.pallas_call(
        _rope_table_kernel,
        grid=(s_max // tm,),
        in_specs=[pl.BlockSpec((1, LANES), lambda i: (0, 0))],
        out_specs=[pl.BlockSpec((tm, LANES), lambda i: (i, 0))] * 2,
        out_shape=[jax.ShapeDtypeStruct((s_max, LANES), F32)] * 2,
        compiler_params=_params(("arbitrary",)),
        name="rope_tables",
    )(inv_lane)


def _rpb_expand_kernel(r_ref, s_ref, m_ref, o_ref):
    o_ref[...] = jnp.dot(r_ref[...], s_ref[...], preferred_element_type=F32,
                         precision=lax.Precision.HIGHEST) + m_ref[...]


def _natten_bias(rpb_l):
    heads = rpb_l.shape[0]
    nr, nc = 2 * NA_ROWS - 1, 2 * NA_COLS - 1
    nc_pad = 32
    col = jnp.arange(GRID_W)
    col_start = jnp.clip(col - NA_COLS // 2, 0, GRID_W - NA_COLS)
    cc = jnp.arange(GRID_W)[None, :]
    in_win = (cc >= col_start[:, None]) & (cc < col_start[:, None] + NA_COLS)
    dc = cc - col[:, None] + (NA_COLS - 1)
    shift = ((dc[None] == jnp.arange(nc_pad)[:, None, None]) & in_win[None]).astype(F32)
    shift = shift.reshape(nc_pad, GRID_W * GRID_W)
    negmask = jnp.where(in_win, 0.0, NEG_INF).astype(F32).reshape(1, GRID_W * GRID_W)
    rows = heads * nr
    rows_pad = -(-rows // SUBLANES) * SUBLANES
    r2 = jnp.zeros((rows_pad, nc_pad), F32).at[:rows, :nc].set(rpb_l.reshape(rows, nc).astype(F32))
    tiles = pl.pallas_call(
        _rpb_expand_kernel,
        out_shape=jax.ShapeDtypeStruct((rows_pad, GRID_W * GRID_W), F32),
        name="rpb_expand",
    )(r2, shift, negmask)
    tiles = tiles[:rows].reshape(heads, nr, GRID_W, GRID_W)
    sel = jnp.arange(NA_ROWS)[:, None] + jnp.arange(NA_ROWS)[None, :]
    var = tiles[:, sel]
    var = jnp.transpose(var, (0, 1, 3, 2, 4))
    return var.reshape(heads, NA_ROWS, GRID_W, NA_ROWS * GRID_W)


def _norm_mod(x, g, sc, sh):
    ms = jnp.mean(x * x, axis=-1, keepdims=True)
    y = x * lax.rsqrt(ms + NORM_EPS) * g
    return y * (1.0 + sc) + sh


def _inproj_kernel(x_ref, g_ref, sc_ref, sh_ref, w_ref, h_ref, o_ref, h_sc):
    @pl.when(pl.program_id(1) == 0)
    def _():
        h = _norm_mod(x_ref[...], g_ref[...], sc_ref[0], sh_ref[0]).astype(BF16)
        h_sc[...] = h
        h_ref[...] = h

    o_ref[...] = _dot(h_sc[...], w_ref[0]).astype(BF16)


def _inproj(x, g, mod_rows, w_bf, layer, seqs, sc_idx, sh_idx):
    t, d = x.shape
    n = w_bf.shape[2]
    tm = _tile(seqs.s1, 512)
    tn = _tile(n, 512)
    mod_spec = lambda which: pl.BlockSpec(
        (1, 1, d), lambda i, j: (seqs.seq_of_tile(i, tm) * 6 + which, 0, 0))
    return pl.pallas_call(
        _inproj_kernel,
        grid=(t // tm, n // tn),
        in_specs=[
            pl.BlockSpec((tm, d), lambda i, j: (i, 0)),
            pl.BlockSpec((1, d), lambda i, j: (0, 0)),
            mod_spec(sc_idx),
            mod_spec(sh_idx),
            pl.BlockSpec((1, d, tn), lambda i, j: (layer, 0, j)),
        ],
        out_specs=[
            pl.BlockSpec((tm, d), lambda i, j: (i, 0)),
            pl.BlockSpec((tm, tn), lambda i, j: (i, j)),
        ],
        out_shape=[jax.ShapeDtypeStruct((t, d), BF16), jax.ShapeDtypeStruct((t, n), BF16)],
        scratch_shapes=[pltpu.VMEM((tm, d), BF16)],
        compiler_params=_params(("arbitrary", "arbitrary")),
        name="inproj",
    )(x, g.reshape(1, d), mod_rows, mod_rows, w_bf)


def _rope_kernel(q_ref, k_ref, cos_ref, sin_ref, qo_ref, ko_ref, *, q_scale):
    cos = cos_ref[...]
    sin = sin_ref[...]
    lane = lax.broadcasted_iota(jnp.int32, cos.shape, 1)
    first_half = lane % SW_HEAD_DIM < SW_HEAD_DIM // 2

    def rot(x):
        partner = jnp.where(first_half,
                            pltpu.roll(x, LANES - SW_HEAD_DIM // 2, 1),
                            pltpu.roll(x, SW_HEAD_DIM // 2, 1))
        return x * cos + partner * sin

    for c in range(q_ref.shape[1] // LANES):
        sl = slice(c * LANES, (c + 1) * LANES)
        qo_ref[:, sl] = (rot(q_ref[:, sl].astype(F32)) * q_scale).astype(BF16)
    for c in range(k_ref.shape[1] // LANES):
        sl = slice(c * LANES, (c + 1) * LANES)
        ko_ref[:, sl] = rot(k_ref[:, sl].astype(F32)).astype(BF16)


def _rope(proj, cos_t, sin_t, seqs, q_col, q_w, k_col, k_w):
    t = proj.shape[0]
    tm = _tile(seqs.s1, 512)
    tab = pl.BlockSpec((tm, LANES), lambda i: (seqs.pos_tile(i, tm), 0))
    return pl.pallas_call(
        functools.partial(_rope_kernel, q_scale=SW_HEAD_DIM ** -0.5),
        grid=(t // tm,),
        in_specs=[
            pl.BlockSpec((tm, q_w), lambda i: (i, q_col // q_w)),
            pl.BlockSpec((tm, k_w), lambda i: (i, k_col // k_w)),
            tab, tab,
        ],
        out_specs=[pl.BlockSpec((tm, q_w), lambda i: (i, 0)),
                   pl.BlockSpec((tm, k_w), lambda i: (i, 0))],
        out_shape=[jax.ShapeDtypeStruct((t, q_w), BF16), jax.ShapeDtypeStruct((t, k_w), BF16)],
        compiler_params=_params(("arbitrary",)),
        name="rope",
    )(proj, proj, cos_t, sin_t)


def _natten_kernel(q_ref, k_ref, v_ref, b_ref, o_ref, *, rows, rows_per_step):
    kr = min(NA_ROWS, rows)
    scale = NA_HEAD_DIM ** -0.5
    base = pl.program_id(2) * rows_per_step

    def scores(rr):
        i = base + rr
        row_start = jnp.clip(i - kr // 2, 0, rows - kr)
        off = row_start - i + (NA_ROWS - 1)
        kstart = pl.multiple_of(row_start * GRID_W, GRID_W)
        q = q_ref[rr * GRID_W:(rr + 1) * GRID_W, :]
        ks = k_ref[pl.ds(kstart, kr * GRID_W), :]
        return _dot_nt(q, ks) * scale + b_ref[0, off], kstart

    def finish(rr, s, kstart):
        vs = v_ref[pl.ds(kstart, kr * GRID_W), :]
        m = jnp.max(s, axis=-1, keepdims=True)
        e = jnp.exp(s - m)
        l = jnp.sum(e, axis=-1, keepdims=True)
        o = _dot(e.astype(BF16), vs) / l
        o_ref[rr * GRID_W:(rr + 1) * GRID_W, :] = o.astype(BF16)

    ahead = min(NATTEN_AHEAD, rows_per_step)
    pending = [scores(rr) for rr in range(ahead)]
    for rr in range(rows_per_step):
        if rr + ahead < rows_per_step:
            pending.append(scores(rr + ahead))
        finish(rr, *pending.pop(0))


def _natten(proj, bias, tok0, n_seq, s, width, heads):
    rows = s // GRID_W
    assert rows >= NA_ROWS
    rps = _tile(rows, 16)
    tq = rps * GRID_W
    nh = width // NA_HEAD_DIM
    qb0 = tok0 // tq
    sb0 = tok0 // s
    return pl.pallas_call(
        functools.partial(_natten_kernel, rows=rows, rows_per_step=rps),
        grid=(n_seq, heads, rows // rps),
        in_specs=[
            pl.BlockSpec((tq, NA_HEAD_DIM), lambda b, h, r: (qb0 + b * (rows // rps) + r, h)),
            pl.BlockSpec((s, NA_HEAD_DIM), lambda b, h, r: (sb0 + b, nh + h)),
            pl.BlockSpec((s, NA_HEAD_DIM), lambda b, h, r: (sb0 + b, 2 * nh + h)),
            pl.BlockSpec((1, NA_ROWS, GRID_W, NA_ROWS * GRID_W), lambda b, h, r: (h, 0, 0, 0)),
        ],
        out_specs=pl.BlockSpec((tq, NA_HEAD_DIM), lambda b, h, r: (b * (rows // rps) + r, h)),
        out_shape=jax.ShapeDtypeStruct((n_seq * s, width), BF16),
        compiler_params=_params(("arbitrary", "arbitrary", "arbitrary")),
        name="natten",
    )(proj, proj, proj, bias)


def _swa_kernel(sink_ref, q_ref, kp_ref, kc_ref, kn_ref, vp_ref, vc_ref, vn_ref, o_ref,
                *, seqs, kv_heads):
    n = pl.program_id(0)
    pos_blk = seqs.pos_tile(n, SW_BLOCK)
    n_blk = jnp.where(n < seqs.s0 // SW_BLOCK, seqs.s0 // SW_BLOCK, seqs.s1 // SW_BLOCK)
    is_first = pos_blk == 0
    is_last = pos_blk == n_blk - 1

    band = 3 * SW_BLOCK
    mpos = lax.broadcasted_iota(jnp.int32, (band, SW_BLOCK), 0)
    qpos = lax.broadcasted_iota(jnp.int32, (band, SW_BLOCK), 1)
    rel = mpos - SW_BLOCK - qpos
    valid = (jnp.abs(rel) <= SW_WINDOW)
    valid = valid & ((mpos >= SW_BLOCK) | jnp.logical_not(is_first))
    valid = valid & ((mpos < 2 * SW_BLOCK) | jnp.logical_not(is_last))

    kb = jnp.concatenate([kp_ref[...], kc_ref[...], kn_ref[...]], axis=0).astype(F32)
    vb = jnp.concatenate([vp_ref[...], vc_ref[...], vn_ref[...]], axis=0).astype(F32)
    vb_t = vb.T.astype(BF16)
    low = lax.broadcasted_iota(jnp.int32, (band, LANES), 1) < SW_HEAD_DIM
    qlane_low = lax.broadcasted_iota(jnp.int32, (SW_BLOCK, LANES), 1) < SW_HEAD_DIM

    for k in range(kv_heads):
        kpair = kb[:, (k // 2) * LANES:(k // 2 + 1) * LANES]
        kroll = pltpu.roll(kpair, SW_HEAD_DIM, 1)
        k2 = (jnp.where(low, kpair, kroll) if k % 2 == 0 else jnp.where(low, kroll, kpair))
        k2 = k2.astype(BF16)
        v_t = vb_t[k * SW_HEAD_DIM:(k + 1) * SW_HEAD_DIM, :]

        qs = []
        for g in range(SW_GROUP):
            hq = k * SW_GROUP + g
            q2 = q_ref[:, (hq // 2) * LANES:(hq // 2 + 1) * LANES]
            keep = qlane_low if hq % 2 == 0 else jnp.logical_not(qlane_low)
            qs.append(jnp.where(keep, q2, jnp.zeros_like(q2)))
        qm = jnp.concatenate(qs, axis=0)
        s_t = _dot_nt(k2, qm)
        for pp in range(SW_GROUP // 2):
            halves = []
            for g in (2 * pp, 2 * pp + 1):
                sink = sink_ref[k * SW_GROUP + g]
                s = jnp.where(valid, s_t[:, g * SW_BLOCK:(g + 1) * SW_BLOCK], NEG_INF)
                m = jnp.maximum(jnp.max(s, axis=0, keepdims=True), sink)
                e = jnp.exp(s - m)
                denom = jnp.sum(e, axis=0, keepdims=True) + jnp.exp(sink - m)
                halves.append(_dot(v_t, e.astype(BF16)) / denom)
            o_t = jnp.concatenate(halves, axis=0)
            c0 = (k * SW_GROUP // 2 + pp) * LANES
            o_ref[:, c0:c0 + LANES] = o_t.T.astype(BF16)


def _swa(qr, kr, proj, sinks_l, seqs, v_col, q_w, kv_w):
    t = qr.shape[0]
    nblk = t // SW_BLOCK
    kv_heads = kv_w // SW_HEAD_DIM
    assert kv_heads % 2 == 0 and kv_w % LANES == 0

    def prev(n, s):
        return jnp.maximum(n - 1, 0)

    def nxt(n, s):
        return jnp.minimum(n + 1, nblk - 1)

    vcb = v_col // kv_w
    grid_spec = pltpu.PrefetchScalarGridSpec(
        num_scalar_prefetch=1,
        grid=(nblk,),
        in_specs=[
            pl.BlockSpec((SW_BLOCK, q_w), lambda n, s: (n, 0)),
            pl.BlockSpec((SW_BLOCK, kv_w), lambda n, s: (prev(n, s), 0)),
            pl.BlockSpec((SW_BLOCK, kv_w), lambda n, s: (n, 0)),
            pl.BlockSpec((SW_BLOCK, kv_w), lambda n, s: (nxt(n, s), 0)),
            pl.BlockSpec((SW_BLOCK, kv_w), lambda n, s: (prev(n, s), vcb)),
            pl.BlockSpec((SW_BLOCK, kv_w), lambda n, s: (n, vcb)),
            pl.BlockSpec((SW_BLOCK, kv_w), lambda n, s: (nxt(n, s), vcb)),
        ],
        out_specs=pl.BlockSpec((SW_BLOCK, q_w), lambda n, s: (n, 0)),
    )
    return pl.pallas_call(
        functools.partial(_swa_kernel, seqs=seqs, kv_heads=kv_heads),
        grid_spec=grid_spec,
        out_shape=jax.ShapeDtypeStruct((t, q_w), BF16),
        compiler_params=_params(("arbitrary",)),
        name="swa",
    )(sinks_l.astype(F32), qr, kr, kr, kr, proj, proj, proj)


def _merge_kernel(h_ref, oa_ref, ob_ref, wga_ref, wgb_ref, wa_ref, wb_ref, o_ref):
    h = h_ref[...]
    ga = jax.nn.sigmoid(_dot(h, wga_ref[0]))
    gb = jax.nn.sigmoid(_dot(h, wgb_ref[0]))
    m = ga * _dot(oa_ref[...], wa_ref[0]) + gb * _dot(ob_ref[...], wb_ref[0])
    o_ref[...] = m.astype(BF16)


def _merge(h, oa, ob, wg_bf, wa_bf, wb_bf, layer):
    t, d = h.shape
    wa_w, wb_w = oa.shape[1], ob.shape[1]
    tm = _tile(t, 512)
    tn = _tile(d, 512)
    nj = d // tn
    return pl.pallas_call(
        _merge_kernel,
        grid=(t // tm, nj),
        in_specs=[
            pl.BlockSpec((tm, d), lambda i, j: (i, 0)),
            pl.BlockSpec((tm, wa_w), lambda i, j: (i, 0)),
            pl.BlockSpec((tm, wb_w), lambda i, j: (i, 0)),
            pl.BlockSpec((1, d, tn), lambda i, j: (layer, 0, j)),
            pl.BlockSpec((1, d, tn), lambda i, j: (layer, 0, nj + j)),
            pl.BlockSpec((1, wa_w, tn), lambda i, j: (layer, 0, j)),
            pl.BlockSpec((1, wb_w, tn), lambda i, j: (layer, 0, j)),
        ],
        out_specs=pl.BlockSpec((tm, tn), lambda i, j: (i, j)),
        out_shape=jax.ShapeDtypeStruct((t, d), BF16),
        compiler_params=_params(("arbitrary", "arbitrary")),
        name="merge",
    )(h, oa, ob, wg_bf, wg_bf, wa_bf, wb_bf)


def _matmul_kernel(x_ref, w_ref, o_ref):
    o_ref[...] = _dot(x_ref[...], w_ref[0]).astype(o_ref.dtype)


def _matmul(x, w, layer, tm_pref=512, tn_pref=1024):
    t, k = x.shape
    n = w.shape[2]
    tm, tn = _tile(t, tm_pref), _tile(n, tn_pref)
    return pl.pallas_call(
        _matmul_kernel,
        grid=(t // tm, n // tn),
        in_specs=[pl.BlockSpec((tm, k), lambda i, j: (i, 0)),
                  pl.BlockSpec((1, k, tn), lambda i, j: (layer, 0, j))],
        out_specs=pl.BlockSpec((tm, tn), lambda i, j: (i, j)),
        out_shape=jax.ShapeDtypeStruct((t, n), BF16),
        compiler_params=_params(("arbitrary", "arbitrary")),
        name="matmul",
    )(x, w)


def _rms(x, g):
    ms = jnp.mean(x * x, axis=-1, keepdims=True)
    return x * lax.rsqrt(ms + NORM_EPS) * g


def _postmix_kernel(x_ref, y_ref, gp_ref, gt_ref, gf_ref, sc_ref, sh_ref, wr_ref, br_ref,
                    x1_ref, h2_ref, idx_ref, wt_ref, *, n_experts):
    x1 = x_ref[...] + gt_ref[0] * _rms(y_ref[...].astype(F32), gp_ref[...])
    x1_ref[...] = x1
    h2 = _rms(x1, gf_ref[...]) * (1.0 + sc_ref[0]) + sh_ref[0]
    h2_ref[...] = _pack_halves(h2)
    logits = jnp.dot(h2, wr_ref[...], preferred_element_type=F32,
                     precision=lax.Precision.HIGHEST) + br_ref[...]
    lane = lax.broadcasted_iota(jnp.int32, logits.shape, 1)
    work = jnp.where(lane < n_experts, logits, -jnp.inf)
    idx_out = jnp.zeros(logits.shape, jnp.int32)
    val_out = jnp.zeros(logits.shape, F32)
    vals = []
    for k in range(TOP_K):
        m = jnp.max(work, axis=-1, keepdims=True)
        sel = jnp.min(jnp.where(work == m, lane, LANES), axis=-1, keepdims=True)
        idx_out = jnp.where(lane == k, sel, idx_out)
        vals.append(m)
        work = jnp.where(lane == sel, -jnp.inf, work)
    es = [jnp.exp(v - vals[0]) for v in vals]
    tot = es[0]
    for e in es[1:]:
        tot = tot + e
    for k in range(TOP_K):
        val_out = jnp.where(lane == k, es[k] / tot, val_out)
    idx_ref[...] = idx_out
    wt_ref[...] = val_out


def _postmix(x, y, g_post, g_ffn, mod_rows, w_router, b_router, seqs, gt_idx, sc_idx, sh_idx):
    t, d = x.shape
    e = w_router.shape[1]
    tm = _tile(seqs.s1, 256)
    wr = jnp.zeros((d, LANES), F32).at[:, :e].set(w_router)
    br = jnp.zeros((1, LANES), F32).at[0, :e].set(b_router)
    mod_spec = lambda which: pl.BlockSpec(
        (1, 1, d), lambda i: (seqs.seq_of_tile(i, tm) * 6 + which, 0, 0))
    row = pl.BlockSpec((tm, d), lambda i: (i, 0))
    vec = pl.BlockSpec((1, d), lambda i: (0, 0))
    small = pl.BlockSpec((tm, LANES), lambda i: (i, 0))
    return pl.pallas_call(
        functools.partial(_postmix_kernel, n_experts=e),
        grid=(t // tm,),
        in_specs=[row, row, vec, mod_spec(gt_idx), vec, mod_spec(sc_idx), mod_spec(sh_idx),
                  pl.BlockSpec((d, LANES), lambda i: (0, 0)),
                  pl.BlockSpec((1, LANES), lambda i: (0, 0))],
        out_specs=[row, pl.BlockSpec((tm, d // 2), lambda i: (i, 0)), small, small],
        out_shape=[jax.ShapeDtypeStruct((t, d), F32), jax.ShapeDtypeStruct((t, d // 2), jnp.uint32),
                   jax.ShapeDtypeStruct((t, LANES), jnp.int32),
                   jax.ShapeDtypeStruct((t, LANES), F32)],
        compiler_params=_params(("arbitrary",)),
        name="postmix",
    )(x, y, g_post.reshape(1, d), mod_rows, g_ffn.reshape(1, d), mod_rows, mod_rows, wr, br)


def _rank_kernel(idx_ref, rank_ref, cnt_ref, carry):
    @pl.when(pl.program_id(0) == 0)
    def _():
        carry[...] = jnp.zeros_like(carry)

    idx = idx_ref[...]
    tb = idx.shape[0]
    lane = lax.broadcasted_iota(jnp.int32, idx.shape, 1)
    hits = [lane == idx[:, k:k + 1] for k in range(TOP_K)]
    onehot = jnp.zeros(idx.shape, F32)
    for hit in hits:
        onehot = onehot + hit.astype(F32)
    r = lax.broadcasted_iota(jnp.int32, (tb, tb), 0)
    c = lax.broadcasted_iota(jnp.int32, (tb, tb), 1)
    tri = (c < r).astype(BF16)
    before = _dot(tri, onehot.astype(BF16)) + carry[...]
    rank = jnp.zeros(idx.shape, F32)
    for k, hit in enumerate(hits):
        rk = jnp.sum(jnp.where(hit, before, 0.0), axis=-1, keepdims=True)
        rank = jnp.where(lane == k, rk, rank)
    rank_ref[...] = rank.astype(jnp.int32)
    carry[...] = carry[...] + jnp.sum(onehot, axis=0, keepdims=True)
    cnt_ref[...] = jnp.broadcast_to(carry[...], cnt_ref.shape).astype(jnp.int32)


def _ranks(idx):
    t = idx.shape[0]
    tb = _tile(t, 256)
    return pl.pallas_call(
        _rank_kernel,
        grid=(t // tb,),
        in_specs=[pl.BlockSpec((tb, LANES), lambda i: (i, 0))],
        out_specs=[pl.BlockSpec((tb, LANES), lambda i: (i, 0)),
                   pl.BlockSpec((SUBLANES, LANES), lambda i: (0, 0))],
        out_shape=[jax.ShapeDtypeStruct((t, LANES), jnp.int32),
                   jax.ShapeDtypeStruct((SUBLANES, LANES), jnp.int32)],
        scratch_shapes=[pltpu.VMEM((1, LANES), F32)],
        compiler_params=_params(("arbitrary",)),
        name="ranks",
    )(idx)


def _row_copy(src_hbm, row, dst, slot, sem):
    return pltpu.make_async_copy(src_hbm.at[pl.ds(row, 1), :], dst.at[pl.ds(slot, 1), :], sem)


def _start_rows(src_hbm, row_of, dst, sem, n):
    def body(c, carry):
        for u in range(DMA_UNROLL):
            i = c * DMA_UNROLL + u
            _row_copy(src_hbm, row_of(i), dst, i, sem).start(priority=u % 2)
        return carry

    lax.fori_loop(0, n // DMA_UNROLL, body, 0)


def _wait_rows(src_hbm, dst, sem, n):
    def body(c, carry):
        for _ in range(DMA_UNROLL):
            _row_copy(src_hbm, 0, dst, 0, sem).wait()
        return carry

    lax.fori_loop(0, n // DMA_UNROLL, body, 0)


def _dispatch_kernel(dest_ref, zstart_ref, used_ref, h_ref, xs_hbm, zero_sc, sem, zsem,
                     *, n_exp, tm):
    i = pl.program_id(0)
    tc = h_ref.shape[0]

    @pl.when(i == 0)
    def _():
        zero_sc[...] = jnp.zeros_like(zero_sc)

        def zero_tile(first_row):
            start = pl.multiple_of(first_row, tm)
            cp = pltpu.make_async_copy(zero_sc, xs_hbm.at[pl.ds(start, tm), :], zsem)
            cp.start()
            cp.wait()

        for e in range(n_exp):
            zero_tile(zstart_ref[e])

        def tail(r, carry):
            zero_tile(r * tm)
            return carry

        lax.fori_loop(used_ref[0], xs_hbm.shape[0] // tm, tail, 0)

    def body(c, carry):
        for u in range(DMA_UNROLL):
            t = c * DMA_UNROLL + u
            for k in range(TOP_K):
                pltpu.make_async_copy(h_ref.at[pl.ds(t, 1), :],
                                      xs_hbm.at[pl.ds(dest_ref[0, k, t], 1), :],
                                      sem).start(priority=k % 2)
        return carry

    lax.fori_loop(0, tc // DMA_UNROLL, body, 0)

    def drain(c, carry):
        for _ in range(DMA_UNROLL * TOP_K):
            pltpu.make_async_copy(h_ref.at[pl.ds(0, 1), :], xs_hbm.at[pl.ds(0, 1), :], sem).wait()
        return carry

    lax.fori_loop(0, tc // DMA_UNROLL, drain, 0)


def _dispatch(h2p, dest, pad_ends, tiles_used, n_rows, tm):
    t, half = h2p.shape
    n_exp = pad_ends.shape[0]
    tc = _tile(t, 256)
    assert tc % DMA_UNROLL == 0
    nt = t // tc
    dest_t = dest.reshape(nt, tc, TOP_K).transpose(0, 2, 1)
    zstart = jnp.maximum(pad_ends - tm, 0).astype(jnp.int32)
    grid_spec = pltpu.PrefetchScalarGridSpec(
        num_scalar_prefetch=0,
        grid=(nt,),
        in_specs=[
            pl.BlockSpec((1, TOP_K, tc), lambda i: (i, 0, 0), memory_space=pltpu.SMEM),
            pl.BlockSpec(memory_space=pltpu.SMEM),
            pl.BlockSpec(memory_space=pltpu.SMEM),
            pl.BlockSpec((tc, half), lambda i: (i, 0)),
        ],
        out_specs=pl.BlockSpec(memory_space=pl.ANY),
        scratch_shapes=[pltpu.VMEM((tm, half), jnp.uint32),
                        pltpu.SemaphoreType.DMA(()), pltpu.SemaphoreType.DMA(())],
    )
    return pl.pallas_call(
        functools.partial(_dispatch_kernel, n_exp=n_exp, tm=tm),
        grid_spec=grid_spec,
        out_shape=jax.ShapeDtypeStruct((n_rows, half), jnp.uint32),
        compiler_params=_params(("arbitrary",)),
        name="dispatch",
    )(dest_t, zstart, tiles_used, h2p)


def _cast_weights_on_expert_change(e_ref, r, used, w_refs, w_sc):
    changed = (r == 0) | (e_ref[r] != e_ref[jnp.maximum(r - 1, 0)])

    @pl.when(changed & (r < used))
    def _():
        k = w_refs[0].shape[2]
        rows = min(CAST_ROWS, k)

        def body(c, carry):
            sl = pl.ds(pl.multiple_of(c * rows, rows), rows)
            for i, w_ref in enumerate(w_refs):
                w_sc[i, sl, :] = w_ref[0, 0, sl, :].astype(BF16)
            return carry

        lax.fori_loop(0, k // rows, body, 0)


def _gateup_kernel(e_ref, src_ref, used_ref, x_ref, wg_ref, wu_ref, bg_ref, bu_ref, o_ref, w_sc):
    r = pl.program_id(1)
    used = used_ref[0]
    _cast_weights_on_expert_change(e_ref, r, used, (wg_ref, wu_ref), w_sc)

    @pl.when(r < used)
    def _():
        lo, hi = _unpack_halves(x_ref[...])
        x = jnp.concatenate([lo.astype(BF16), hi.astype(BF16)], axis=1)
        gate = _dot(x, w_sc[0]) + bg_ref[0, 0]
        up = _dot(x, w_sc[1]) + bu_ref[0, 0]
        gate = jnp.minimum(gate, SWIGLU_LIMIT)
        up = jnp.clip(up, -SWIGLU_LIMIT, SWIGLU_LIMIT)
        glu = gate * jax.nn.sigmoid(gate * SWIGLU_ALPHA)
        o_ref[...] = ((up + 1.0) * glu).astype(BF16)

    @pl.when(r >= used)
    def _():
        o_ref[...] = jnp.zeros_like(o_ref)


def _gateup(xsp, w_gate_up, b_gate_up, layer, tile_expert, tile_src, tiles_used, tm):
    n_rows, half = xsp.shape
    _, n_exp, d, two_f = w_gate_up.shape
    f = two_f // 2
    tn = _tile(f, 512)
    nj = f // tn
    nt = n_rows // tm
    b4 = b_gate_up.reshape(-1, n_exp, 1, two_f)
    grid_spec = pltpu.PrefetchScalarGridSpec(
        num_scalar_prefetch=3,
        grid=(nj, nt),
        in_specs=[
            pl.BlockSpec((tm, half), lambda j, r, e, s, u: (s[r], 0)),
            pl.BlockSpec((1, 1, d, tn), lambda j, r, e, s, u: (layer, e[r], 0, j)),
            pl.BlockSpec((1, 1, d, tn), lambda j, r, e, s, u: (layer, e[r], 0, nj + j)),
            pl.BlockSpec((1, 1, 1, tn), lambda j, r, e, s, u: (layer, e[r], 0, j)),
            pl.BlockSpec((1, 1, 1, tn), lambda j, r, e, s, u: (layer, e[r], 0, nj + j)),
        ],
        out_specs=pl.BlockSpec((tm, tn), lambda j, r, e, s, u: (r, j)),
        scratch_shapes=[pltpu.VMEM((2, d, tn), BF16)],
    )
    return pl.pallas_call(
        _gateup_kernel,
        grid_spec=grid_spec,
        out_shape=jax.ShapeDtypeStruct((n_rows, f), BF16),
        compiler_params=_params(("arbitrary", "arbitrary")),
        name="gateup",
    )(tile_expert, tile_src, tiles_used, xsp, w_gate_up, w_gate_up, b4, b4)


def _down_kernel(e_ref, src_ref, used_ref, a_ref, wl_ref, wh_ref, bl_ref, bh_ref, o_ref, w_sc):
    r = pl.program_id(1)
    used = used_ref[0]
    _cast_weights_on_expert_change(e_ref, r, used, (wl_ref, wh_ref), w_sc)

    @pl.when(r < used)
    def _():
        a = a_ref[...]
        lo = _dot(a, w_sc[0]) + bl_ref[0, 0]
        hi = _dot(a, w_sc[1]) + bh_ref[0, 0]
        o_ref[...] = _pack_pair(lo, hi)

    @pl.when(r >= used)
    def _():
        o_ref[...] = jnp.zeros_like(o_ref)


def _down(a, w_down, b_down, layer, tile_expert, tile_src, tiles_used, tm):
    n_rows, f = a.shape
    _, n_exp, _, d = w_down.shape
    half = d // 2
    tn = _tile(half, 1024)
    nj = half // tn
    nt = n_rows // tm
    b4 = b_down.reshape(-1, n_exp, 1, d)
    grid_spec = pltpu.PrefetchScalarGridSpec(
        num_scalar_prefetch=3,
        grid=(nj, nt),
        in_specs=[
            pl.BlockSpec((tm, f), lambda j, r, e, s, u: (s[r], 0)),
            pl.BlockSpec((1, 1, f, tn), lambda j, r, e, s, u: (layer, e[r], 0, j)),
            pl.BlockSpec((1, 1, f, tn), lambda j, r, e, s, u: (layer, e[r], 0, nj + j)),
            pl.BlockSpec((1, 1, 1, tn), lambda j, r, e, s, u: (layer, e[r], 0, j)),
            pl.BlockSpec((1, 1, 1, tn), lambda j, r, e, s, u: (layer, e[r], 0, nj + j)),
        ],
        out_specs=pl.BlockSpec((tm, tn), lambda j, r, e, s, u: (r, j)),
        scratch_shapes=[pltpu.VMEM((2, f, tn), BF16)],
    )
    return pl.pallas_call(
        _down_kernel,
        grid_spec=grid_spec,
        out_shape=jax.ShapeDtypeStruct((n_rows, half), jnp.uint32),
        compiler_params=_params(("arbitrary", "arbitrary")),
        name="down",
    )(tile_expert, tile_src, tiles_used, a, w_down, w_down, b4, b4)


def _combine_kernel(dest_ref, nxt_ref, ys_hbm, wt_ref, x_ref, g_ref, gt_ref, o_ref, buf, sem,
                    *, n):
    i = pl.program_id(0)
    tc = x_ref.shape[0]
    half = buf.shape[3]

    def start(idx_ref, slot):
        for k in range(TOP_K):
            _start_rows(ys_hbm, lambda t, k=k: idx_ref[0, k, t], buf.at[slot, k], sem.at[slot], tc)

    @pl.when(i == 0)
    def _():
        start(dest_ref, 0)

    @pl.when(i + 1 < n)
    def _():
        start(nxt_ref, (i + 1) % 2)

    slot = i % 2
    for k in range(TOP_K):
        _wait_rows(ys_hbm, buf.at[slot, k], sem.at[slot], tc)
    wt = wt_ref[...]
    y_lo = jnp.zeros((tc, half), F32)
    y_hi = jnp.zeros((tc, half), F32)
    for k in range(TOP_K):
        lo, hi = _unpack_halves(buf[slot, k])
        y_lo = y_lo + wt[:, k:k + 1] * lo
        y_hi = y_hi + wt[:, k:k + 1] * hi
    y = jnp.concatenate([y_lo, y_hi], axis=1)
    o_ref[...] = x_ref[...] + gt_ref[0] * _rms(y, g_ref[...])


def _combine(ysp, dest, wt, x1, g_post, mod_rows, seqs, gt_idx, tok0, n_tok):
    t, d = x1.shape
    tc = _tile(seqs.s1, 128)
    assert tc % DMA_UNROLL == 0 and tok0 % tc == 0 and n_tok % tc == 0
    nt = n_tok // tc
    t0 = tok0 // tc
    dest_t = dest.reshape(t // tc, tc, TOP_K).transpose(0, 2, 1)
    grid_spec = pltpu.PrefetchScalarGridSpec(
        num_scalar_prefetch=0,
        grid=(nt,),
        in_specs=[
            pl.BlockSpec((1, TOP_K, tc), lambda i: (t0 + i, 0, 0), memory_space=pltpu.SMEM),
            pl.BlockSpec((1, TOP_K, tc), lambda i: (t0 + jnp.minimum(i + 1, nt - 1), 0, 0),
                         memory_space=pltpu.SMEM),
            pl.BlockSpec(memory_space=pl.ANY),
            pl.BlockSpec((tc, LANES), lambda i: (t0 + i, 0)),
            pl.BlockSpec((tc, d), lambda i: (t0 + i, 0)),
            pl.BlockSpec((1, d), lambda i: (0, 0)),
            pl.BlockSpec((1, 1, d),
                         lambda i: (seqs.seq_of_tile(t0 + i, tc) * 6 + gt_idx, 0, 0)),
        ],
        out_specs=pl.BlockSpec((tc, d), lambda i: (i, 0)),
        scratch_shapes=[pltpu.VMEM((2, TOP_K, tc, d // 2), jnp.uint32),
                        pltpu.SemaphoreType.DMA((2,))],
    )
    return pl.pallas_call(
        functools.partial(_combine_kernel, n=nt),
        grid_spec=grid_spec,
        out_shape=jax.ShapeDtypeStruct((n_tok, d), F32),
        compiler_params=_params(("arbitrary",)),
        name="combine",
    )(dest_t, dest_t, ysp, wt, x1, g_post.reshape(1, d), mod_rows)


def _moe(h2p, idx, w_gate_up, b_gate_up, w_down, b_down, layer, tm):
    t = h2p.shape[0]
    n_exp = w_gate_up.shape[1]
    rank, cnt = _ranks(idx)
    counts = cnt[0, :n_exp]
    padded = (counts + tm - 1) // tm * tm
    pad_ends = jnp.cumsum(padded)
    pad_starts = pad_ends - padded
    n_assign = t * TOP_K
    nt = -(-n_assign // tm) + n_exp
    n_rows = nt * tm
    top_idx = idx[:, :TOP_K]
    dest = (pad_starts[top_idx] + rank[:, :TOP_K]).astype(jnp.int32)
    tiles_used = (pad_ends[-1] // tm).astype(jnp.int32).reshape(1)
    tile_ids = jnp.arange(nt, dtype=jnp.int32)
    tile_src = jnp.minimum(tile_ids, tiles_used[0] - 1)
    tile_expert = jnp.sum(pad_ends[None, :] <= (tile_src * tm)[:, None], axis=1)
    tile_expert = jnp.minimum(tile_expert, n_exp - 1).astype(jnp.int32)

    xsp = _dispatch(h2p, dest, pad_ends, tiles_used, n_rows, tm)
    a = _gateup(xsp, w_gate_up, b_gate_up, layer, tile_expert, tile_src, tiles_used, tm)
    ysp = _down(a, w_down, b_down, layer, tile_expert, tile_src, tiles_used, tm)
    return ysp, dest


def kernel(x_prompt, x_sample, c_prompt, c_sample, w_ada, b_ada, g_mix_pre, g_mix_post, g_ffn_pre, g_ffn_post, w_in, w_branch_gate, rpb, sinks, w_proj_a, w_proj_b, w_o, w_router, b_router, w_gate_up, b_gate_up, w_down, b_down):
    b0, s0, d = x_prompt.shape
    b1, s1, _ = x_sample.shape
    assert b0 == 1 and s0 % s1 == 0
    depth = w_ada.shape[0]
    seqs = _Seqs(s0, b1, s1)
    na_w = w_proj_a.shape[1]
    sw_w = w_proj_b.shape[1]
    kv_w = (w_in.shape[2] - 3 * na_w - sw_w) // 2
    na_heads = na_w // NA_HEAD_DIM
    q_col, k_col, v_col = 3 * na_w, 3 * na_w + sw_w, 3 * na_w + sw_w + kv_w
    moe_tm = _tile(seqs.total * TOP_K, 512)

    x = jnp.concatenate([x_prompt.reshape(s0, d), x_sample.reshape(b1 * s1, d)], axis=0)
    c8 = jnp.zeros((SUBLANES, d), F32).at[:1 + b1].set(
        jnp.concatenate([c_prompt, c_sample], axis=0))
    mod = _adaln(c8, w_ada, b_ada)
    cos_t, sin_t = _rope_tables(s0)
    w_in_bf = w_in.astype(BF16)
    wg_bf = w_branch_gate.astype(BF16)
    wa_bf = w_proj_a.astype(BF16)
    wb_bf = w_proj_b.astype(BF16)
    wo_bf = w_o.astype(BF16)

    for l in range(depth):
        mod_rows = mod[l].reshape(SUBLANES * 6, 1, d)
        h, proj = _inproj(x, g_mix_pre[l], mod_rows, w_in_bf, l, seqs, 1, 0)
        bias = _natten_bias(rpb[l])
        oa = jnp.concatenate([
            _natten(proj, bias, 0, 1, s0, na_w, na_heads),
            _natten(proj, bias, s0, b1, s1, na_w, na_heads)], axis=0)
        qr, kr = _rope(proj, cos_t, sin_t, seqs, q_col, sw_w, k_col, kv_w)
        ob = _swa(qr, kr, proj, sinks[l], seqs, v_col, sw_w, kv_w)
        merged = _merge(h, oa, ob, wg_bf, wa_bf, wb_bf, l)
        y = _matmul(merged, wo_bf, l)
        x1, h2p, idx, wt = _postmix(x, y, g_mix_post[l], g_ffn_pre[l], mod_rows,
                                    w_router[l], b_router[l], seqs, 2, 4, 3)
        ysp, dest = _moe(h2p, idx, w_gate_up, b_gate_up, w_down, b_down, l, moe_tm)
        combine = functools.partial(_combine, ysp, dest, wt, x1, g_ffn_post[l], mod_rows, seqs, 5)
        if l + 1 < depth:
            x = combine(0, seqs.total)
        else:
            y_prompt = combine(0, s0)
            y_sample = combine(s0, b1 * s1)

    return (y_prompt.reshape(b0, s0, d), y_sample.reshape(b1, s1, d))
```

```python
import functools

import jax
import jax.numpy as jnp
from jax import lax
from jax.experimental import pallas as pl
from jax.experimental.pallas import tpu as pltpu

GRID_W = 64
NA_HEAD_DIM = 128
NA_ROWS = 8
NA_COLS = 16
SW_HEAD_DIM = 64
SW_GROUP = 8
SW_WINDOW = 128
SW_BLOCK = 128
ROPE_THETA = 10000.0
TOP_K = 4
SWIGLU_LIMIT = 7.0
SWIGLU_ALPHA = 1.702
NORM_EPS = 1e-6
NEG_INF = -1e30

NATTEN_AHEAD = 2
DMA_UNROLL = 8
CAST_ROWS = 256

LANES = 128
SUBLANES = 8
VMEM_LIMIT_BYTES = 56 * 1024 * 1024

F32 = jnp.float32
BF16 = jnp.bfloat16


def _params(semantics):
    return pltpu.CompilerParams(dimension_semantics=semantics,
                                vmem_limit_bytes=VMEM_LIMIT_BYTES)


def _tile(n, pref):
    t = min(n, pref)
    while n % t:
        t //= 2
    return t


def _dot(a, b):
    return jnp.dot(a, b, preferred_element_type=F32)


def _dot_nt(a, b):
    return lax.dot_general(a, b, (((1,), (1,)), ((), ())), preferred_element_type=F32)


def _pack_pair(lo, hi):
    lo = pltpu.bitcast(lo.astype(BF16).astype(F32), jnp.uint32)
    hi = pltpu.bitcast(hi.astype(BF16).astype(F32), jnp.uint32)
    return (hi & jnp.uint32(0xFFFF0000)) | (lo >> 16)


def _pack_halves(x):
    h = x.shape[1] // 2
    return _pack_pair(x[:, :h], x[:, h:])


def _unpack_halves(w):
    lo = pltpu.bitcast(w << 16, F32)
    hi = pltpu.bitcast(w & jnp.uint32(0xFFFF0000), F32)
    return lo, hi


class _Seqs:
    def __init__(self, s0, nb1, s1):
        self.s0, self.nb1, self.s1 = s0, nb1, s1
        self.total = s0 + nb1 * s1

    def seq_of_tile(self, i, tile):
        n0 = self.s0 // tile
        per = self.s1 // tile
        return jnp.where(i < n0, 0, 1 + (i - n0) // per)

    def pos_tile(self, i, tile):
        n0 = self.s0 // tile
        per = self.s1 // tile
        return jnp.where(i < n0, i, (i - n0) % per)


def _adaln_kernel(c_ref, w_ref, b_ref, o_ref):
    c = c_ref[...]
    a = (c * jax.nn.sigmoid(c)).astype(BF16)
    o_ref[0] = _dot(a, w_ref[0].astype(BF16)) + b_ref[0]


def _adaln(c8, w_ada, b_ada):
    depth, d, n = w_ada.shape
    tn = _tile(n, 512)
    return pl.pallas_call(
        _adaln_kernel,
        grid=(depth, n // tn),
        in_specs=[
            pl.BlockSpec((SUBLANES, d), lambda l, j: (0, 0)),
            pl.BlockSpec((1, d, tn), lambda l, j: (l, 0, j)),
            pl.BlockSpec((1, 1, tn), lambda l, j: (l, 0, j)),
        ],
        out_specs=pl.BlockSpec((1, SUBLANES, tn), lambda l, j: (l, 0, j)),
        out_shape=jax.ShapeDtypeStruct((depth, SUBLANES, n), F32),
        compiler_params=_params(("arbitrary", "arbitrary")),
        name="adaln",
    )(c8, w_ada, b_ada.reshape(depth, 1, n))


def _rope_table_kernel(inv_ref, cos_ref, sin_ref):
    tm = cos_ref.shape[0]
    base = pl.program_id(0) * tm
    pos = (base + lax.broadcasted_iota(jnp.int32, (tm, LANES), 0)).astype(F32)
    ang = pos * inv_ref[...]
    lane = lax.broadcasted_iota(jnp.int32, (tm, LANES), 1)
    sign = jnp.where(lane % SW_HEAD_DIM < SW_HEAD_DIM // 2, -1.0, 1.0)
    cos_ref[...] = jnp.cos(ang)
    sin_ref[...] = jnp.sin(ang) * sign


def _rope_tables(s_max):
    half = SW_HEAD_DIM // 2
    inv_freq = ROPE_THETA ** (-jnp.arange(0, SW_HEAD_DIM, 2, dtype=F32) / SW_HEAD_DIM)
    inv_lane = jnp.tile(inv_freq, LANES // half).reshape(1, LANES)
    tm = _tile(s_max, 512)
    return pl.pallas_call(
        _rope_table_kernel,
        grid=(s_max // tm,),
        in_specs=[pl.BlockSpec((1, LANES), lambda i: (0, 0))],
        out_specs=[pl.BlockSpec((tm, LANES), lambda i: (i, 0))] * 2,
        out_shape=[jax.ShapeDtypeStruct((s_max, LANES), F32)] * 2,
        compiler_params=_params(("arbitrary",)),
        name="rope_tables",
    )(inv_lane)


def _rpb_expand_kernel(r_ref, s_ref, m_ref, o_ref):
    o_ref[...] = jnp.dot(r_ref[...], s_ref[...], preferred_element_type=F32,
                         precision=lax.Precision.HIGHEST) + m_ref[...]


def _natten_bias(rpb_l):
    heads = rpb_l.shape[0]
    nr, nc = 2 * NA_ROWS - 1, 2 * NA_COLS - 1
    nc_pad = 32
    col = jnp.arange(GRID_W)
    col_start = jnp.clip(col - NA_COLS // 2, 0, GRID_W - NA_COLS)
    cc = jnp.arange(GRID_W)[None, :]
    in_win = (cc >= col_start[:, None]) & (cc < col_start[:, None] + NA_COLS)
    dc = cc - col[:, None] + (NA_COLS - 1)
    shift = ((dc[None] == jnp.arange(nc_pad)[:, None, None]) & in_win[None]).astype(F32)
    shift = shift.reshape(nc_pad, GRID_W * GRID_W)
    negmask = jnp.where(in_win, 0.0, NEG_INF).astype(F32).reshape(1, GRID_W * GRID_W)
    rows = heads * nr
    rows_pad = -(-rows // SUBLANES) * SUBLANES
    r2 = jnp.zeros((rows_pad, nc_pad), F32).at[:rows, :nc].set(rpb_l.reshape(rows, nc).astype(F32))
    tiles = pl.pallas_call(
        _rpb_expand_kernel,
        out_shape=jax.ShapeDtypeStruct((rows_pad, GRID_W * GRID_W), F32),
        name="rpb_expand",
    )(r2, shift, negmask)
    tiles = tiles[:rows].reshape(heads, nr, GRID_W, GRID_W)
    sel = jnp.arange(NA_ROWS)[:, None] + jnp.arange(NA_ROWS)[None, :]
    var = tiles[:, sel]
    var = jnp.transpose(var, (0, 1, 3, 2, 4))
    return var.reshape(heads, NA_ROWS, GRID_W, NA_ROWS * GRID_W)


def _norm_mod(x, g, sc, sh):
    ms = jnp.mean(x * x, axis=-1, keepdims=True)
    y = x * lax.rsqrt(ms + NORM_EPS) * g
    return y * (1.0 + sc) + sh


def _inproj_kernel(x_ref, g_ref, sc_ref, sh_ref, w_ref, h_ref, o_ref, h_sc):
    @pl.when(pl.program_id(1) == 0)
    def _():
        h = _norm_mod(x_ref[...], g_ref[...], sc_ref[0], sh_ref[0]).astype(BF16)
        h_sc[...] = h
        h_ref[...] = h

    o_ref[...] = _dot(h_sc[...], w_ref[0]).astype(BF16)


def _inproj(x, g, mod_rows, w_bf, layer, seqs, sc_idx, sh_idx):
    t, d = x.shape
    n = w_bf.shape[2]
    tm = _tile(seqs.s1, 512)
    tn = _tile(n, 512)
    mod_spec = lambda which: pl.BlockSpec(
        (1, 1, d), lambda i, j: (seqs.seq_of_tile(i, tm) * 6 + which, 0, 0))
    return pl.pallas_call(
        _inproj_kernel,
        grid=(t // tm, n // tn),
        in_specs=[
            pl.BlockSpec((tm, d), lambda i, j: (i, 0)),
            pl.BlockSpec((1, d), lambda i, j: (0, 0)),
            mod_spec(sc_idx),
            mod_spec(sh_idx),
            pl.BlockSpec((1, d, tn), lambda i, j: (layer, 0, j)),
        ],
        out_specs=[
            pl.BlockSpec((tm, d), lambda i, j: (i, 0)),
            pl.BlockSpec((tm, tn), lambda i, j: (i, j)),
        ],
        out_shape=[jax.ShapeDtypeStruct((t, d), BF16), jax.ShapeDtypeStruct((t, n), BF16)],
        scratch_shapes=[pltpu.VMEM((tm, d), BF16)],
        compiler_params=_params(("arbitrary", "arbitrary")),
        name="inproj",
    )(x, g.reshape(1, d), mod_rows, mod_rows, w_bf)


def _rope_kernel(q_ref, k_ref, cos_ref, sin_ref, qo_ref, ko_ref, *, q_scale):
    cos = cos_ref[...]
    sin = sin_ref[...]
    lane = lax.broadcasted_iota(jnp.int32, cos.shape, 1)
    first_half = lane % SW_HEAD_DIM < SW_HEAD_DIM // 2

    def rot(x):
        partner = jnp.where(first_half,
                            pltpu.roll(x, LANES - SW_HEAD_DIM // 2, 1),
                            pltpu.roll(x, SW_HEAD_DIM // 2, 1))
        return x * cos + partner * sin

    for c in range(q_ref.shape[1] // LANES):
        sl = slice(c * LANES, (c + 1) * LANES)
        qo_ref[:, sl] = (rot(q_ref[:, sl].astype(F32)) * q_scale).astype(BF16)
    for c in range(k_ref.shape[1] // LANES):
        sl = slice(c * LANES, (c + 1) * LANES)
        ko_ref[:, sl] = rot(k_ref[:, sl].astype(F32)).astype(BF16)


def _rope(proj, cos_t, sin_t, seqs, q_col, q_w, k_col, k_w):
    t = proj.shape[0]
    tm = _tile(seqs.s1, 512)
    tab = pl.BlockSpec((tm, LANES), lambda i: (seqs.pos_tile(i, tm), 0))
    return pl.pallas_call(
        functools.partial(_rope_kernel, q_scale=SW_HEAD_DIM ** -0.5),
        grid=(t // tm,),
        in_specs=[
            pl.BlockSpec((tm, q_w), lambda i: (i, q_col // q_w)),
            pl.BlockSpec((tm, k_w), lambda i: (i, k_col // k_w)),
            tab, tab,
        ],
        out_specs=[pl.BlockSpec((tm, q_w), lambda i: (i, 0)),
                   pl.BlockSpec((tm, k_w), lambda i: (i, 0))],
        out_shape=[jax.ShapeDtypeStruct((t, q_w), BF16), jax.ShapeDtypeStruct((t, k_w), BF16)],
        compiler_params=_params(("arbitrary",)),
        name="rope",
    )(proj, proj, cos_t, sin_t)


def _natten_kernel(q_ref, k_ref, v_ref, b_ref, o_ref, *, rows, rows_per_step):
    kr = min(NA_ROWS, rows)
    scale = NA_HEAD_DIM ** -0.5
    base = pl.program_id(2) * rows_per_step

    def scores(rr):
        i = base + rr
        row_start = jnp.clip(i - kr // 2, 0, rows - kr)
        off = row_start - i + (NA_ROWS - 1)
        kstart = pl.multiple_of(row_start * GRID_W, GRID_W)
        q = q_ref[rr * GRID_W:(rr + 1) * GRID_W, :]
        ks = k_ref[pl.ds(kstart, kr * GRID_W), :]
        return _dot_nt(q, ks) * scale + b_ref[0, off], kstart

    def finish(rr, s, kstart):
        vs = v_ref[pl.ds(kstart, kr * GRID_W), :]
        m = jnp.max(s, axis=-1, keepdims=True)
        e = jnp.exp(s - m)
        l = jnp.sum(e, axis=-1, keepdims=True)
        o = _dot(e.astype(BF16), vs) / l
        o_ref[rr * GRID_W:(rr + 1) * GRID_W, :] = o.astype(BF16)

    ahead = min(NATTEN_AHEAD, rows_per_step)
    pending = [scores(rr) for rr in range(ahead)]
    for rr in range(rows_per_step):
        if rr + ahead < rows_per_step:
            pending.append(scores(rr + ahead))
        finish(rr, *pending.pop(0))


def _natten(proj, bias, tok0, n_seq, s, width, heads):
    rows = s // GRID_W
    assert rows >= NA_ROWS
    rps = _tile(rows, 16)
    tq = rps * GRID_W
    nh = width // NA_HEAD_DIM
    qb0 = tok0 // tq
    sb0 = tok0 // s
    return pl.pallas_call(
        functools.partial(_natten_kernel, rows=rows, rows_per_step=rps),
        grid=(n_seq, heads, rows // rps),
        in_specs=[
            pl.BlockSpec((tq, NA_HEAD_DIM), lambda b, h, r: (qb0 + b * (rows // rps) + r, h)),
            pl.BlockSpec((s, NA_HEAD_DIM), lambda b, h, r: (sb0 + b, nh + h)),
            pl.BlockSpec((s, NA_HEAD_DIM), lambda b, h, r: (sb0 + b, 2 * nh + h)),
            pl.BlockSpec((1, NA_ROWS, GRID_W, NA_ROWS * GRID_W), lambda b, h, r: (h, 0, 0, 0)),
        ],
        out_specs=pl.BlockSpec((tq, NA_HEAD_DIM), lambda b, h, r: (b * (rows // rps) + r, h)),
        out_shape=jax.ShapeDtypeStruct((n_seq * s, width), BF16),
        compiler_params=_params(("arbitrary", "arbitrary", "arbitrary")),
        name="natten",
    )(proj, proj, proj, bias)


def _swa_kernel(sink_ref, q_ref, kp_ref, kc_ref, kn_ref, vp_ref, vc_ref, vn_ref, o_ref,
                *, seqs, kv_heads):
    n = pl.program_id(0)
    pos_blk = seqs.pos_tile(n, SW_BLOCK)
    n_blk = jnp.where(n < seqs.s0 // SW_BLOCK, seqs.s0 // SW_BLOCK, seqs.s1 // SW_BLOCK)
    is_first = pos_blk == 0
    is_last = pos_blk == n_blk - 1

    band = 3 * SW_BLOCK
    mpos = lax.broadcasted_iota(jnp.int32, (band, SW_BLOCK), 0)
    qpos = lax.broadcasted_iota(jnp.int32, (band, SW_BLOCK), 1)
    rel = mpos - SW_BLOCK - qpos
    valid = (jnp.abs(rel) <= SW_WINDOW)
    valid = valid & ((mpos >= SW_BLOCK) | jnp.logical_not(is_first))
    valid = valid & ((mpos < 2 * SW_BLOCK) | jnp.logical_not(is_last))

    kb = jnp.concatenate([kp_ref[...], kc_ref[...], kn_ref[...]], axis=0).astype(F32)
    vb = jnp.concatenate([vp_ref[...], vc_ref[...], vn_ref[...]], axis=0).astype(F32)
    vb_t = vb.T.astype(BF16)
    low = lax.broadcasted_iota(jnp.int32, (band, LANES), 1) < SW_HEAD_DIM
    qlane_low = lax.broadcasted_iota(jnp.int32, (SW_BLOCK, LANES), 1) < SW_HEAD_DIM

    for k in range(kv_heads):
        kpair = kb[:, (k // 2) * LANES:(k // 2 + 1) * LANES]
        kroll = pltpu.roll(kpair, SW_HEAD_DIM, 1)
        k2 = (jnp.where(low, kpair, kroll) if k % 2 == 0 else jnp.where(low, kroll, kpair))
        k2 = k2.astype(BF16)
        v_t = vb_t[k * SW_HEAD_DIM:(k + 1) * SW_HEAD_DIM, :]

        qs = []
        for g in range(SW_GROUP):
            hq = k * SW_GROUP + g
            q2 = q_ref[:, (hq // 2) * LANES:(hq // 2 + 1) * LANES]
            keep = qlane_low if hq % 2 == 0 else jnp.logical_not(qlane_low)
            qs.append(jnp.where(keep, q2, jnp.zeros_like(q2)))
        qm = jnp.concatenate(qs, axis=0)
        s_t = _dot_nt(k2, qm)
        for pp in range(SW_GROUP // 2):
            halves = []
            for g in (2 * pp, 2 * pp + 1):
                sink = sink_ref[k * SW_GROUP + g]
                s = jnp.where(valid, s_t[:, g * SW_BLOCK:(g + 1) * SW_BLOCK], NEG_INF)
                m = jnp.maximum(jnp.max(s, axis=0, keepdims=True), sink)
                e = jnp.exp(s - m)
                denom = jnp.sum(e, axis=0, keepdims=True) + jnp.exp(sink - m)
                halves.append(_dot(v_t, e.astype(BF16)) / denom)
            o_t = jnp.concatenate(halves, axis=0)
            c0 = (k * SW_GROUP // 2 + pp) * LANES
            o_ref[:, c0:c0 + LANES] = o_t.T.astype(BF16)


def _swa(qr, kr, proj, sinks_l, seqs, v_col, q_w, kv_w):
    t = qr.shape[0]
    nblk = t // SW_BLOCK
    kv_heads = kv_w // SW_HEAD_DIM
    assert kv_heads % 2 == 0 and kv_w % LANES == 0

    def prev(n, s):
        return jnp.maximum(n - 1, 0)

    def nxt(n, s):
        return jnp.minimum(n + 1, nblk - 1)

    vcb = v_col // kv_w
    grid_spec = pltpu.PrefetchScalarGridSpec(
        num_scalar_prefetch=1,
        grid=(nblk,),
        in_specs=[
            pl.BlockSpec((SW_BLOCK, q_w), lambda n, s: (n, 0)),
            pl.BlockSpec((SW_BLOCK, kv_w), lambda n, s: (prev(n, s), 0)),
            pl.BlockSpec((SW_BLOCK, kv_w), lambda n, s: (n, 0)),
            pl.BlockSpec((SW_BLOCK, kv_w), lambda n, s: (nxt(n, s), 0)),
            pl.BlockSpec((SW_BLOCK, kv_w), lambda n, s: (prev(n, s), vcb)),
            pl.BlockSpec((SW_BLOCK, kv_w), lambda n, s: (n, vcb)),
            pl.BlockSpec((SW_BLOCK, kv_w), lambda n, s: (nxt(n, s), vcb)),
        ],
        out_specs=pl.BlockSpec((SW_BLOCK, q_w), lambda n, s: (n, 0)),
    )
    return pl.pallas_call(
        functools.partial(_swa_kernel, seqs=seqs, kv_heads=kv_heads),
        grid_spec=grid_spec,
        out_shape=jax.ShapeDtypeStruct((t, q_w), BF16),
        compiler_params=_params(("arbitrary",)),
        name="swa",
    )(sinks_l.astype(F32), qr, kr, kr, kr, proj, proj, proj)


def _merge_kernel(h_ref, oa_ref, ob_ref, wga_ref, wgb_ref, wa_ref, wb_ref, o_ref):
    h = h_ref[...]
    ga = jax.nn.sigmoid(_dot(h, wga_ref[0]))
    gb = jax.nn.sigmoid(_dot(h, wgb_ref[0]))
    m = ga * _dot(oa_ref[...], wa_ref[0]) + gb * _dot(ob_ref[...], wb_ref[0])
    o_ref[...] = m.astype(BF16)


def _merge(h, oa, ob, wg_bf, wa_bf, wb_bf, layer):
    t, d = h.shape
    wa_w, wb_w = oa.shape[1], ob.shape[1]
    tm = _tile(t, 512)
    tn = _tile(d, 512)
    nj = d // tn
    return pl.pallas_call(
        _merge_kernel,
        grid=(t // tm, nj),
        in_specs=[
            pl.BlockSpec((tm, d), lambda i, j: (i, 0)),
            pl.BlockSpec((tm, wa_w), lambda i, j: (i, 0)),
            pl.BlockSpec((tm, wb_w), lambda i, j: (i, 0)),
            pl.BlockSpec((1, d, tn), lambda i, j: (layer, 0, j)),
            pl.BlockSpec((1, d, tn), lambda i, j: (layer, 0, nj + j)),
            pl.BlockSpec((1, wa_w, tn), lambda i, j: (layer, 0, j)),
            pl.BlockSpec((1, wb_w, tn), lambda i, j: (layer, 0, j)),
        ],
        out_specs=pl.BlockSpec((tm, tn), lambda i, j: (i, j)),
        out_shape=jax.ShapeDtypeStruct((t, d), BF16),
        compiler_params=_params(("arbitrary", "arbitrary")),
        name="merge",
    )(h, oa, ob, wg_bf, wg_bf, wa_bf, wb_bf)


def _matmul_kernel(x_ref, w_ref, o_ref):
    o_ref[...] = _dot(x_ref[...], w_ref[0]).astype(o_ref.dtype)


def _matmul(x, w, layer, tm_pref=512, tn_pref=1024):
    t, k = x.shape
    n = w.shape[2]
    tm, tn = _tile(t, tm_pref), _tile(n, tn_pref)
    return pl.pallas_call(
        _matmul_kernel,
        grid=(t // tm, n // tn),
        in_specs=[pl.BlockSpec((tm, k), lambda i, j: (i, 0)),
                  pl.BlockSpec((1, k, tn), lambda i, j: (layer, 0, j))],
        out_specs=pl.BlockSpec((tm, tn), lambda i, j: (i, j)),
        out_shape=jax.ShapeDtypeStruct((t, n), BF16),
        compiler_params=_params(("arbitrary", "arbitrary")),
        name="matmul",
    )(x, w)


def _rms(x, g):
    ms = jnp.mean(x * x, axis=-1, keepdims=True)
    return x * lax.rsqrt(ms + NORM_EPS) * g


def _postmix_kernel(x_ref, y_ref, gp_ref, gt_ref, gf_ref, sc_ref, sh_ref, wr_ref, br_ref,
                    x1_ref, h2_ref, idx_ref, wt_ref, *, n_experts):
    x1 = x_ref[...] + gt_ref[0] * _rms(y_ref[...].astype(F32), gp_ref[...])
    x1_ref[...] = x1
    h2 = _rms(x1, gf_ref[...]) * (1.0 + sc_ref[0]) + sh_ref[0]
    h2_ref[...] = _pack_halves(h2)
    h_hi = h2.astype(BF16)
    h_lo = (h2 - h_hi.astype(F32)).astype(BF16)
    hi_terms = _dot(h_hi, wr_ref[...])
    logits = (hi_terms[:, :LANES] + (hi_terms[:, LANES:] + _dot(h_lo, wr_ref[:, :LANES]))
              + br_ref[...])
    lane = lax.broadcasted_iota(jnp.int32, logits.shape, 1)
    work = jnp.where(lane < n_experts, logits, -jnp.inf)
    idx_out = jnp.zeros(logits.shape, jnp.int32)
    val_out = jnp.zeros(logits.shape, F32)
    vals = []
    for k in range(TOP_K):
        m = jnp.max(work, axis=-1, keepdims=True)
        sel = jnp.min(jnp.where(work == m, lane, LANES), axis=-1, keepdims=True)
        idx_out = jnp.where(lane == k, sel, idx_out)
        vals.append(m)
        work = jnp.where(lane == sel, -jnp.inf, work)
    es = [jnp.exp(v - vals[0]) for v in vals]
    tot = es[0]
    for e in es[1:]:
        tot = tot + e
    for k in range(TOP_K):
        val_out = jnp.where(lane == k, es[k] / tot, val_out)
    idx_ref[...] = idx_out
    wt_ref[...] = val_out


def _postmix(x, y, g_post, g_ffn, mod_rows, w_router, b_router, seqs, gt_idx, sc_idx, sh_idx):
    t, d = x.shape
    e = w_router.shape[1]
    tm = _tile(seqs.s1, 256)
    w_pad = jnp.zeros((d, LANES), F32).at[:, :e].set(w_router)
    w_hi = w_pad.astype(BF16)
    w_lo = (w_pad - w_hi.astype(F32)).astype(BF16)
    wr = jnp.concatenate([w_hi, w_lo], axis=1)
    br = jnp.zeros((1, LANES), F32).at[0, :e].set(b_router)
    mod_spec = lambda which: pl.BlockSpec(
        (1, 1, d), lambda i: (seqs.seq_of_tile(i, tm) * 6 + which, 0, 0))
    row = pl.BlockSpec((tm, d), lambda i: (i, 0))
    vec = pl.BlockSpec((1, d), lambda i: (0, 0))
    small = pl.BlockSpec((tm, LANES), lambda i: (i, 0))
    return pl.pallas_call(
        functools.partial(_postmix_kernel, n_experts=e),
        grid=(t // tm,),
        in_specs=[row, row, vec, mod_spec(gt_idx), vec, mod_spec(sc_idx), mod_spec(sh_idx),
                  pl.BlockSpec((d, 2 * LANES), lambda i: (0, 0)),
                  pl.BlockSpec((1, LANES), lambda i: (0, 0))],
        out_specs=[row, pl.BlockSpec((tm, d // 2), lambda i: (i, 0)), small, small],
        out_shape=[jax.ShapeDtypeStruct((t, d), F32), jax.ShapeDtypeStruct((t, d // 2), jnp.uint32),
                   jax.ShapeDtypeStruct((t, LANES), jnp.int32),
                   jax.ShapeDtypeStruct((t, LANES), F32)],
        compiler_params=_params(("arbitrary",)),
        name="postmix",
    )(x, y, g_post.reshape(1, d), mod_rows, g_ffn.reshape(1, d), mod_rows, mod_rows, wr, br)


def _rank_kernel(idx_ref, rank_ref, cnt_ref, carry):
    @pl.when(pl.program_id(0) == 0)
    def _():
        carry[...] = jnp.zeros_like(carry)

    idx = idx_ref[...]
    tb = idx.shape[0]
    lane = lax.broadcasted_iota(jnp.int32, idx.shape, 1)
    hits = [lane == idx[:, k:k + 1] for k in range(TOP_K)]
    onehot = jnp.zeros(idx.shape, F32)
    for hit in hits:
        onehot = onehot + hit.astype(F32)
    r = lax.broadcasted_iota(jnp.int32, (tb, tb), 0)
    c = lax.broadcasted_iota(jnp.int32, (tb, tb), 1)
    tri = (c < r).astype(BF16)
    before = _dot(tri, onehot.astype(BF16)) + carry[...]
    rank = jnp.zeros(idx.shape, F32)
    for k, hit in enumerate(hits):
        rk = jnp.sum(jnp.where(hit, before, 0.0), axis=-1, keepdims=True)
        rank = jnp.where(lane == k, rk, rank)
    rank_ref[...] = rank.astype(jnp.int32)
    carry[...] = carry[...] + jnp.sum(onehot, axis=0, keepdims=True)
    cnt_ref[...] = jnp.broadcast_to(carry[...], cnt_ref.shape).astype(jnp.int32)


def _ranks(idx):
    t = idx.shape[0]
    tb = _tile(t, 256)
    return pl.pallas_call(
        _rank_kernel,
        grid=(t // tb,),
        in_specs=[pl.BlockSpec((tb, LANES), lambda i: (i, 0))],
        out_specs=[pl.BlockSpec((tb, LANES), lambda i: (i, 0)),
                   pl.BlockSpec((SUBLANES, LANES), lambda i: (0, 0))],
        out_shape=[jax.ShapeDtypeStruct((t, LANES), jnp.int32),
                   jax.ShapeDtypeStruct((SUBLANES, LANES), jnp.int32)],
        scratch_shapes=[pltpu.VMEM((1, LANES), F32)],
        compiler_params=_params(("arbitrary",)),
        name="ranks",
    )(idx)


def _row_copy(src_hbm, row, dst, slot, sem):
    return pltpu.make_async_copy(src_hbm.at[pl.ds(row, 1), :], dst.at[pl.ds(slot, 1), :], sem)


def _wait_rows(src_hbm, dst, sem, n):
    def body(c, carry):
        for _ in range(DMA_UNROLL):
            _row_copy(src_hbm, 0, dst, 0, sem).wait()
        return carry

    lax.fori_loop(0, n // DMA_UNROLL, body, 0)


def _dispatch_kernel(dest_ref, zstart_ref, used_ref, h_ref, xs_hbm, zero_sc, sem, zsem,
                     *, n_exp, tm):
    i = pl.program_id(0)
    tc = h_ref.shape[0]

    @pl.when(i == 0)
    def _():
        zero_sc[...] = jnp.zeros_like(zero_sc)

        def zero_tile(first_row):
            start = pl.multiple_of(first_row, tm)
            cp = pltpu.make_async_copy(zero_sc, xs_hbm.at[pl.ds(start, tm), :], zsem)
            cp.start()
            cp.wait()

        for e in range(n_exp):
            zero_tile(zstart_ref[e])

        def tail(r, carry):
            zero_tile(r * tm)
            return carry

        lax.fori_loop(used_ref[0], xs_hbm.shape[0] // tm, tail, 0)

    def body(c, carry):
        for u in range(DMA_UNROLL):
            t = c * DMA_UNROLL + u
            for k in range(TOP_K):
                pltpu.make_async_copy(h_ref.at[pl.ds(t, 1), :],
                                      xs_hbm.at[pl.ds(dest_ref[0, k, t], 1), :],
                                      sem).start(priority=k % 2)
        return carry

    lax.fori_loop(0, tc // DMA_UNROLL, body, 0)

    def drain(c, carry):
        for _ in range(DMA_UNROLL * TOP_K):
            pltpu.make_async_copy(h_ref.at[pl.ds(0, 1), :], xs_hbm.at[pl.ds(0, 1), :], sem).wait()
        return carry

    lax.fori_loop(0, tc // DMA_UNROLL, drain, 0)


def _dispatch(h2p, dest, pad_ends, tiles_used, n_rows, tm):
    t, half = h2p.shape
    n_exp = pad_ends.shape[0]
    tc = _tile(t, 256)
    assert tc % DMA_UNROLL == 0
    nt = t // tc
    dest_t = dest.reshape(nt, tc, TOP_K).transpose(0, 2, 1)
    zstart = jnp.maximum(pad_ends - tm, 0).astype(jnp.int32)
    grid_spec = pltpu.PrefetchScalarGridSpec(
        num_scalar_prefetch=0,
        grid=(nt,),
        in_specs=[
            pl.BlockSpec((1, TOP_K, tc), lambda i: (i, 0, 0), memory_space=pltpu.SMEM),
            pl.BlockSpec(memory_space=pltpu.SMEM),
            pl.BlockSpec(memory_space=pltpu.SMEM),
            pl.BlockSpec((tc, half), lambda i: (i, 0)),
        ],
        out_specs=pl.BlockSpec(memory_space=pl.ANY),
        scratch_shapes=[pltpu.VMEM((tm, half), jnp.uint32),
                        pltpu.SemaphoreType.DMA(()), pltpu.SemaphoreType.DMA(())],
    )
    return pl.pallas_call(
        functools.partial(_dispatch_kernel, n_exp=n_exp, tm=tm),
        grid_spec=grid_spec,
        out_shape=jax.ShapeDtypeStruct((n_rows, half), jnp.uint32),
        compiler_params=_params(("arbitrary",)),
        name="dispatch",
    )(dest_t, zstart, tiles_used, h2p)


def _cast_weights_on_expert_change(e_ref, r, used, w_refs, w_sc):
    changed = (r == 0) | (e_ref[r] != e_ref[jnp.maximum(r - 1, 0)])

    @pl.when(changed & (r < used))
    def _():
        k = w_refs[0].shape[2]
        rows = min(CAST_ROWS, k)

        def body(c, carry):
            sl = pl.ds(pl.multiple_of(c * rows, rows), rows)
            for i, w_ref in enumerate(w_refs):
                w_sc[i, sl, :] = w_ref[0, 0, sl, :].astype(BF16)
            return carry

        lax.fori_loop(0, k // rows, body, 0)


def _gateup_kernel(e_ref, src_ref, used_ref, x_ref, wg_ref, wu_ref, bg_ref, bu_ref, o_ref, w_sc):
    r = pl.program_id(1)
    used = used_ref[0]
    _cast_weights_on_expert_change(e_ref, r, used, (wg_ref, wu_ref), w_sc)

    @pl.when(r < used)
    def _():
        lo, hi = _unpack_halves(x_ref[...])
        x = jnp.concatenate([lo.astype(BF16), hi.astype(BF16)], axis=1)
        gate = _dot(x, w_sc[0]) + bg_ref[0, 0]
        up = _dot(x, w_sc[1]) + bu_ref[0, 0]
        gate = jnp.minimum(gate, SWIGLU_LIMIT)
        up = jnp.clip(up, -SWIGLU_LIMIT, SWIGLU_LIMIT)
        glu = gate * jax.nn.sigmoid(gate * SWIGLU_ALPHA)
        o_ref[...] = ((up + 1.0) * glu).astype(BF16)

    @pl.when(r >= used)
    def _():
        o_ref[...] = jnp.zeros_like(o_ref)


def _gateup(xsp, w_gate_up, b_gate_up, layer, tile_expert, tile_src, tiles_used, tm):
    n_rows, half = xsp.shape
    _, n_exp, d, two_f = w_gate_up.shape
    f = two_f // 2
    tn = _tile(f, 512)
    nj = f // tn
    nt = n_rows // tm
    b4 = b_gate_up.reshape(-1, n_exp, 1, two_f)
    grid_spec = pltpu.PrefetchScalarGridSpec(
        num_scalar_prefetch=3,
        grid=(nj, nt),
        in_specs=[
            pl.BlockSpec((tm, half), lambda j, r, e, s, u: (s[r], 0)),
            pl.BlockSpec((1, 1, d, tn), lambda j, r, e, s, u: (layer, e[r], 0, j)),
            pl.BlockSpec((1, 1, d, tn), lambda j, r, e, s, u: (layer, e[r], 0, nj + j)),
            pl.BlockSpec((1, 1, 1, tn), lambda j, r, e, s, u: (layer, e[r], 0, j)),
            pl.BlockSpec((1, 1, 1, tn), lambda j, r, e, s, u: (layer, e[r], 0, nj + j)),
        ],
        out_specs=pl.BlockSpec((tm, tn), lambda j, r, e, s, u: (r, j)),
        scratch_shapes=[pltpu.VMEM((2, d, tn), BF16)],
    )
    return pl.pallas_call(
        _gateup_kernel,
        grid_spec=grid_spec,
        out_shape=jax.ShapeDtypeStruct((n_rows, f), BF16),
        compiler_params=_params(("arbitrary", "arbitrary")),
        name="gateup",
    )(tile_expert, tile_src, tiles_used, xsp, w_gate_up, w_gate_up, b4, b4)


def _down_kernel(e_ref, src_ref, used_ref, a_ref, wl_ref, wh_ref, bl_ref, bh_ref, o_ref, w_sc):
    r = pl.program_id(1)
    used = used_ref[0]
    _cast_weights_on_expert_change(e_ref, r, used, (wl_ref, wh_ref), w_sc)

    @pl.when(r < used)
    def _():
        a = a_ref[...]
        lo = _dot(a, w_sc[0]) + bl_ref[0, 0]
        hi = _dot(a, w_sc[1]) + bh_ref[0, 0]
        o_ref[...] = _pack_pair(lo, hi)

    @pl.when(r >= used)
    def _():
        o_ref[...] = jnp.zeros_like(o_ref)


def _down(a, w_down, b_down, layer, tile_expert, tile_src, tiles_used, tm):
    n_rows, f = a.shape
    _, n_exp, _, d = w_down.shape
    half = d // 2
    tn = _tile(half, 1024)
    nj = half // tn
    nt = n_rows // tm
    b4 = b_down.reshape(-1, n_exp, 1, d)
    grid_spec = pltpu.PrefetchScalarGridSpec(
        num_scalar_prefetch=3,
        grid=(nj, nt),
        in_specs=[
            pl.BlockSpec((tm, f), lambda j, r, e, s, u: (s[r], 0)),
            pl.BlockSpec((1, 1, f, tn), lambda j, r, e, s, u: (layer, e[r], 0, j)),
            pl.BlockSpec((1, 1, f, tn), lambda j, r, e, s, u: (layer, e[r], 0, nj + j)),
            pl.BlockSpec((1, 1, 1, tn), lambda j, r, e, s, u: (layer, e[r], 0, j)),
            pl.BlockSpec((1, 1, 1, tn), lambda j, r, e, s, u: (layer, e[r], 0, nj + j)),
        ],
        out_specs=pl.BlockSpec((tm, tn), lambda j, r, e, s, u: (r, j)),
        scratch_shapes=[pltpu.VMEM((2, f, tn), BF16)],
    )
    return pl.pallas_call(
        _down_kernel,
        grid_spec=grid_spec,
        out_shape=jax.ShapeDtypeStruct((n_rows, half), jnp.uint32),
        compiler_params=_params(("arbitrary", "arbitrary")),
        name="down",
    )(tile_expert, tile_src, tiles_used, a, w_down, w_down, b4, b4)


def _combine_kernel(dest_ref, nxt_ref, ys_hbm, wt_ref, x_ref, g_ref, gt_ref, o_ref, buf, sem,
                    *, n):
    i = pl.program_id(0)
    tc = x_ref.shape[0]
    half = buf.shape[3]

    def start(idx_ref, slot):
        def body(c, carry):
            for u in range(DMA_UNROLL):
                t = c * DMA_UNROLL + u
                for k in range(TOP_K):
                    _row_copy(ys_hbm, idx_ref[0, k, t], buf.at[slot, k], t,
                              sem.at[slot]).start(priority=k % 2)
            return carry

        lax.fori_loop(0, tc // DMA_UNROLL, body, 0)

    @pl.when(i == 0)
    def _():
        start(dest_ref, 0)

    @pl.when(i + 1 < n)
    def _():
        start(nxt_ref, (i + 1) % 2)

    slot = i % 2
    for k in range(TOP_K):
        _wait_rows(ys_hbm, buf.at[slot, k], sem.at[slot], tc)
    wt = wt_ref[...]
    y_lo = jnp.zeros((tc, half), F32)
    y_hi = jnp.zeros((tc, half), F32)
    for k in range(TOP_K):
        lo, hi = _unpack_halves(buf[slot, k])
        y_lo = y_lo + wt[:, k:k + 1] * lo
        y_hi = y_hi + wt[:, k:k + 1] * hi
    y = jnp.concatenate([y_lo, y_hi], axis=1)
    o_ref[...] = x_ref[...] + gt_ref[0] * _rms(y, g_ref[...])


def _combine(ysp, dest, wt, x1, g_post, mod_rows, seqs, gt_idx, tok0, n_tok):
    t, d = x1.shape
    tc = _tile(seqs.s1, 128)
    assert tc % DMA_UNROLL == 0 and tok0 % tc == 0 and n_tok % tc == 0
    nt = n_tok // tc
    t0 = tok0 // tc
    dest_t = dest.reshape(t // tc, tc, TOP_K).transpose(0, 2, 1)
    grid_spec = pltpu.PrefetchScalarGridSpec(
        num_scalar_prefetch=0,
        grid=(nt,),
        in_specs=[
            pl.BlockSpec((1, TOP_K, tc), lambda i: (t0 + i, 0, 0), memory_space=pltpu.SMEM),
            pl.BlockSpec((1, TOP_K, tc), lambda i: (t0 + jnp.minimum(i + 1, nt - 1), 0, 0),
                         memory_space=pltpu.SMEM),
            pl.BlockSpec(memory_space=pl.ANY),
            pl.BlockSpec((tc, LANES), lambda i: (t0 + i, 0)),
            pl.BlockSpec((tc, d), lambda i: (t0 + i, 0)),
            pl.BlockSpec((1, d), lambda i: (0, 0)),
            pl.BlockSpec((1, 1, d),
                         lambda i: (seqs.seq_of_tile(t0 + i, tc) * 6 + gt_idx, 0, 0)),
        ],
        out_specs=pl.BlockSpec((tc, d), lambda i: (i, 0)),
        scratch_shapes=[pltpu.VMEM((2, TOP_K, tc, d // 2), jnp.uint32),
                        pltpu.SemaphoreType.DMA((2,))],
    )
    return pl.pallas_call(
        functools.partial(_combine_kernel, n=nt),
        grid_spec=grid_spec,
        out_shape=jax.ShapeDtypeStruct((n_tok, d), F32),
        compiler_params=_params(("arbitrary",)),
        name="combine",
    )(dest_t, dest_t, ysp, wt, x1, g_post.reshape(1, d), mod_rows)


def _moe(h2p, idx, w_gate_up, b_gate_up, w_down, b_down, layer, tm):
    t = h2p.shape[0]
    n_exp = w_gate_up.shape[1]
    rank, cnt = _ranks(idx)
    counts = cnt[0, :n_exp]
    padded = (counts + tm - 1) // tm * tm
    pad_ends = jnp.cumsum(padded)
    pad_starts = pad_ends - padded
    n_assign = t * TOP_K
    nt = -(-n_assign // tm) + n_exp
    n_rows = nt * tm
    top_idx = idx[:, :TOP_K]
    dest = (pad_starts[top_idx] + rank[:, :TOP_K]).astype(jnp.int32)
    tiles_used = (pad_ends[-1] // tm).astype(jnp.int32).reshape(1)
    tile_ids = jnp.arange(nt, dtype=jnp.int32)
    tile_src = jnp.minimum(tile_ids, tiles_used[0] - 1)
    tile_expert = jnp.sum(pad_ends[None, :] <= (tile_src * tm)[:, None], axis=1)
    tile_expert = jnp.minimum(tile_expert, n_exp - 1).astype(jnp.int32)

    xsp = _dispatch(h2p, dest, pad_ends, tiles_used, n_rows, tm)
    a = _gateup(xsp, w_gate_up, b_gate_up, layer, tile_expert, tile_src, tiles_used, tm)
    ysp = _down(a, w_down, b_down, layer, tile_expert, tile_src, tiles_used, tm)
    return ysp, dest


def kernel(x_prompt, x_sample, c_prompt, c_sample, w_ada, b_ada, g_mix_pre, g_mix_post, g_ffn_pre, g_ffn_post, w_in, w_branch_gate, rpb, sinks, w_proj_a, w_proj_b, w_o, w_router, b_router, w_gate_up, b_gate_up, w_down, b_down):
    b0, s0, d = x_prompt.shape
    b1, s1, _ = x_sample.shape
    assert b0 == 1 and s0 % s1 == 0
    depth = w_ada.shape[0]
    seqs = _Seqs(s0, b1, s1)
    na_w = w_proj_a.shape[1]
    sw_w = w_proj_b.shape[1]
    kv_w = (w_in.shape[2] - 3 * na_w - sw_w) // 2
    na_heads = na_w // NA_HEAD_DIM
    q_col, k_col, v_col = 3 * na_w, 3 * na_w + sw_w, 3 * na_w + sw_w + kv_w
    moe_tm = _tile(seqs.total * TOP_K, 512)

    x = jnp.concatenate([x_prompt.reshape(s0, d), x_sample.reshape(b1 * s1, d)], axis=0)
    c8 = jnp.zeros((SUBLANES, d), F32).at[:1 + b1].set(
        jnp.concatenate([c_prompt, c_sample], axis=0))
    mod = _adaln(c8, w_ada, b_ada)
    cos_t, sin_t = _rope_tables(s0)
    w_in_bf = w_in.astype(BF16)
    wg_bf = w_branch_gate.astype(BF16)
    wa_bf = w_proj_a.astype(BF16)
    wb_bf = w_proj_b.astype(BF16)
    wo_bf = w_o.astype(BF16)

    for l in range(depth):
        mod_rows = mod[l].reshape(SUBLANES * 6, 1, d)
        h, proj = _inproj(x, g_mix_pre[l], mod_rows, w_in_bf, l, seqs, 1, 0)
        bias = _natten_bias(rpb[l])
        oa = jnp.concatenate([
            _natten(proj, bias, 0, 1, s0, na_w, na_heads),
            _natten(proj, bias, s0, b1, s1, na_w, na_heads)], axis=0)
        qr, kr = _rope(proj, cos_t, sin_t, seqs, q_col, sw_w, k_col, kv_w)
        ob = _swa(qr, kr, proj, sinks[l], seqs, v_col, sw_w, kv_w)
        merged = _merge(h, oa, ob, wg_bf, wa_bf, wb_bf, l)
        y = _matmul(merged, wo_bf, l)
        x1, h2p, idx, wt = _postmix(x, y, g_mix_post[l], g_ffn_pre[l], mod_rows,
                                    w_router[l], b_router[l], seqs, 2, 4, 3)
        ysp, dest = _moe(h2p, idx, w_gate_up, b_gate_up, w_down, b_down, l, moe_tm)
        combine = functools.partial(_combine, ysp, dest, wt, x1, g_ffn_post[l], mod_rows, seqs, 5)
        if l + 1 < depth:
            x = combine(0, seqs.total)
        else:
            y_prompt = combine(0, s0)
            y_sample = combine(s0, b1 * s1)

    return (y_prompt.reshape(b0, s0, d), y_sample.reshape(b1, s1, d))
```

```python
import functools

import jax
import jax.numpy as jnp
from jax import lax
from jax.experimental import pallas as pl
from jax.experimental.pallas import tpu as pltpu

GRID_W = 64
NA_HEAD_DIM = 128
NA_ROWS = 8
NA_COLS = 16
SW_HEAD_DIM = 64
SW_GROUP = 8
SW_WINDOW = 128
SW_BLOCK = 128
ROPE_THETA = 10000.0
TOP_K = 4
SWIGLU_LIMIT = 7.0
SWIGLU_ALPHA = 1.702
NORM_EPS = 1e-6
NEG_INF = -1e30

NATTEN_AHEAD = 2
DMA_UNROLL = 8
CAST_ROWS = 256

LANES = 128
SUBLANES = 8
VMEM_LIMIT_BYTES = 56 * 1024 * 1024

F32 = jnp.float32
BF16 = jnp.bfloat16


def _params(semantics):
    return pltpu.CompilerParams(dimension_semantics=semantics,
                                vmem_limit_bytes=VMEM_LIMIT_BYTES)


def _tile(n, pref):
    t = min(n, pref)
    while n % t:
        t //= 2
    return t


def _dot(a, b):
    return jnp.dot(a, b, preferred_element_type=F32)


def _dot_nt(a, b):
    return lax.dot_general(a, b, (((1,), (1,)), ((), ())), preferred_element_type=F32)


def _pack_pair(lo, hi):
    lo = pltpu.bitcast(lo.astype(BF16).astype(F32), jnp.uint32)
    hi = pltpu.bitcast(hi.astype(BF16).astype(F32), jnp.uint32)
    return (hi & jnp.uint32(0xFFFF0000)) | (lo >> 16)


def _pack_halves(x):
    h = x.shape[1] // 2
    return _pack_pair(x[:, :h], x[:, h:])


def _unpack_halves(w):
    lo = pltpu.bitcast(w << 16, F32)
    hi = pltpu.bitcast(w & jnp.uint32(0xFFFF0000), F32)
    return lo, hi


class _Seqs:
    def __init__(self, s0, nb1, s1):
        self.s0, self.nb1, self.s1 = s0, nb1, s1
        self.total = s0 + nb1 * s1

    def seq_of_tile(self, i, tile):
        n0 = self.s0 // tile
        per = self.s1 // tile
        return jnp.where(i < n0, 0, 1 + (i - n0) // per)

    def pos_tile(self, i, tile):
        n0 = self.s0 // tile
        per = self.s1 // tile
        return jnp.where(i < n0, i, (i - n0) % per)


def _adaln_kernel(c_ref, w_ref, b_ref, o_ref):
    c = c_ref[...]
    a = (c * jax.nn.sigmoid(c)).astype(BF16)
    o_ref[0] = _dot(a, w_ref[0].astype(BF16)) + b_ref[0]


def _adaln(c8, w_ada, b_ada):
    depth, d, n = w_ada.shape
    tn = _tile(n, 512)
    return pl.pallas_call(
        _adaln_kernel,
        grid=(depth, n // tn),
        in_specs=[
            pl.BlockSpec((SUBLANES, d), lambda l, j: (0, 0)),
            pl.BlockSpec((1, d, tn), lambda l, j: (l, 0, j)),
            pl.BlockSpec((1, 1, tn), lambda l, j: (l, 0, j)),
        ],
        out_specs=pl.BlockSpec((1, SUBLANES, tn), lambda l, j: (l, 0, j)),
        out_shape=jax.ShapeDtypeStruct((depth, SUBLANES, n), F32),
        compiler_params=_params(("arbitrary", "arbitrary")),
        name="adaln",
    )(c8, w_ada, b_ada.reshape(depth, 1, n))


def _rope_table_kernel(inv_ref, cos_ref, sin_ref):
    tm = cos_ref.shape[0]
    base = pl.program_id(0) * tm
    pos = (base + lax.broadcasted_iota(jnp.int32, (tm, LANES), 0)).astype(F32)
    ang = pos * inv_ref[...]
    lane = lax.broadcasted_iota(jnp.int32, (tm, LANES), 1)
    sign = jnp.where(lane % SW_HEAD_DIM < SW_HEAD_DIM // 2, -1.0, 1.0)
    cos_ref[...] = jnp.cos(ang)
    sin_ref[...] = jnp.sin(ang) * sign


def _rope_tables(s_max):
    half = SW_HEAD_DIM // 2
    inv_freq = ROPE_THETA ** (-jnp.arange(0, SW_HEAD_DIM, 2, dtype=F32) / SW_HEAD_DIM)
    inv_lane = jnp.tile(inv_freq, LANES // half).reshape(1, LANES)
    tm = _tile(s_max, 512)
    return pl.pallas_call(
        _rope_table_kernel,
        grid=(s_max // tm,),
        in_specs=[pl.BlockSpec((1, LANES), lambda i: (0, 0))],
        out_specs=[pl.BlockSpec((tm, LANES), lambda i: (i, 0))] * 2,
        out_shape=[jax.ShapeDtypeStruct((s_max, LANES), F32)] * 2,
        compiler_params=_params(("arbitrary",)),
        name="rope_tables",
    )(inv_lane)


def _rpb_expand_kernel(r_ref, s_ref, m_ref, o_ref):
    o_ref[...] = jnp.dot(r_ref[...], s_ref[...], preferred_element_type=F32,
                         precision=lax.Precision.HIGHEST) + m_ref[...]


def _natten_bias(rpb_l):
    heads = rpb_l.shape[0]
    nr, nc = 2 * NA_ROWS - 1, 2 * NA_COLS - 1
    nc_pad = 32
    col = jnp.arange(GRID_W)
    col_start = jnp.clip(col - NA_COLS // 2, 0, GRID_W - NA_COLS)
    cc = jnp.arange(GRID_W)[None, :]
    in_win = (cc >= col_start[:, None]) & (cc < col_start[:, None] + NA_COLS)
    dc = cc - col[:, None] + (NA_COLS - 1)
    shift = ((dc[None] == jnp.arange(nc_pad)[:, None, None]) & in_win[None]).astype(F32)
    shift = shift.reshape(nc_pad, GRID_W * GRID_W)
    negmask = jnp.where(in_win, 0.0, NEG_INF).astype(F32).reshape(1, GRID_W * GRID_W)
    rows = heads * nr
    rows_pad = -(-rows // SUBLANES) * SUBLANES
    r2 = jnp.zeros((rows_pad, nc_pad), F32).at[:rows, :nc].set(rpb_l.reshape(rows, nc).astype(F32))
    tiles = pl.pallas_call(
        _rpb_expand_kernel,
        out_shape=jax.ShapeDtypeStruct((rows_pad, GRID_W * GRID_W), F32),
        name="rpb_expand",
    )(r2, shift, negmask)
    tiles = tiles[:rows].reshape(heads, nr, GRID_W, GRID_W)
    sel = jnp.arange(NA_ROWS)[:, None] + jnp.arange(NA_ROWS)[None, :]
    var = tiles[:, sel]
    var = jnp.transpose(var, (0, 1, 3, 2, 4))
    return var.reshape(heads, NA_ROWS, GRID_W, NA_ROWS * GRID_W)


def _norm_mod(x, g, sc, sh):
    ms = jnp.mean(x * x, axis=-1, keepdims=True)
    y = x * lax.rsqrt(ms + NORM_EPS) * g
    return y * (1.0 + sc) + sh


def _prenorm_kernel(x_ref, g_ref, sc_ref, sh_ref, h_ref):
    h_ref[...] = _norm_mod(x_ref[...], g_ref[...], sc_ref[0], sh_ref[0]).astype(BF16)


def _prenorm(x, g, mod_rows, seqs, sc_idx, sh_idx):
    t, d = x.shape
    tm = _tile(seqs.s1, 256)
    mod_spec = lambda which: pl.BlockSpec(
        (1, 1, d), lambda i: (seqs.seq_of_tile(i, tm) * 6 + which, 0, 0))
    row = pl.BlockSpec((tm, d), lambda i: (i, 0))
    return pl.pallas_call(
        _prenorm_kernel,
        grid=(t // tm,),
        in_specs=[row, pl.BlockSpec((1, d), lambda i: (0, 0)), mod_spec(sc_idx), mod_spec(sh_idx)],
        out_specs=row,
        out_shape=jax.ShapeDtypeStruct((t, d), BF16),
        compiler_params=_params(("arbitrary",)),
        name="prenorm",
    )(x, g.reshape(1, d), mod_rows, mod_rows)


def _rope_kernel(q_ref, k_ref, cos_ref, sin_ref, qo_ref, ko_ref, *, q_scale):
    cos = cos_ref[...]
    sin = sin_ref[...]
    lane = lax.broadcasted_iota(jnp.int32, cos.shape, 1)
    first_half = lane % SW_HEAD_DIM < SW_HEAD_DIM // 2

    def rot(x):
        partner = jnp.where(first_half,
                            pltpu.roll(x, LANES - SW_HEAD_DIM // 2, 1),
                            pltpu.roll(x, SW_HEAD_DIM // 2, 1))
        return x * cos + partner * sin

    for c in range(q_ref.shape[1] // LANES):
        sl = slice(c * LANES, (c + 1) * LANES)
        qo_ref[:, sl] = (rot(q_ref[:, sl].astype(F32)) * q_scale).astype(BF16)
    for c in range(k_ref.shape[1] // LANES):
        sl = slice(c * LANES, (c + 1) * LANES)
        ko_ref[:, sl] = rot(k_ref[:, sl].astype(F32)).astype(BF16)


def _rope(proj, cos_t, sin_t, seqs, q_col, q_w, k_col, k_w):
    t = proj.shape[0]
    tm = _tile(seqs.s1, 512)
    tab = pl.BlockSpec((tm, LANES), lambda i: (seqs.pos_tile(i, tm), 0))
    return pl.pallas_call(
        functools.partial(_rope_kernel, q_scale=SW_HEAD_DIM ** -0.5),
        grid=(t // tm,),
        in_specs=[
            pl.BlockSpec((tm, q_w), lambda i: (i, q_col // q_w)),
            pl.BlockSpec((tm, k_w), lambda i: (i, k_col // k_w)),
            tab, tab,
        ],
        out_specs=[pl.BlockSpec((tm, q_w), lambda i: (i, 0)),
                   pl.BlockSpec((tm, k_w), lambda i: (i, 0))],
        out_shape=[jax.ShapeDtypeStruct((t, q_w), BF16), jax.ShapeDtypeStruct((t, k_w), BF16)],
        compiler_params=_params(("arbitrary",)),
        name="rope",
    )(proj, proj, cos_t, sin_t)


def _natten_kernel(q_ref, k_ref, v_ref, b_ref, o_ref, *, rows, rows_per_step):
    kr = min(NA_ROWS, rows)
    scale = NA_HEAD_DIM ** -0.5
    base = pl.program_id(2) * rows_per_step

    def scores(rr):
        i = base + rr
        row_start = jnp.clip(i - kr // 2, 0, rows - kr)
        off = row_start - i + (NA_ROWS - 1)
        kstart = pl.multiple_of(row_start * GRID_W, GRID_W)
        q = q_ref[rr * GRID_W:(rr + 1) * GRID_W, :]
        ks = k_ref[pl.ds(kstart, kr * GRID_W), :]
        return _dot_nt(q, ks) * scale + b_ref[0, off], kstart

    def finish(rr, s, kstart):
        vs = v_ref[pl.ds(kstart, kr * GRID_W), :]
        m = jnp.max(s, axis=-1, keepdims=True)
        e = jnp.exp(s - m)
        l = jnp.sum(e, axis=-1, keepdims=True)
        o = _dot(e.astype(BF16), vs) / l
        o_ref[rr * GRID_W:(rr + 1) * GRID_W, :] = o.astype(BF16)

    ahead = min(NATTEN_AHEAD, rows_per_step)
    pending = [scores(rr) for rr in range(ahead)]
    for rr in range(rows_per_step):
        if rr + ahead < rows_per_step:
            pending.append(scores(rr + ahead))
        finish(rr, *pending.pop(0))


def _natten(proj, bias, tok0, n_seq, s, width, heads):
    rows = s // GRID_W
    assert rows >= NA_ROWS
    rps = _tile(rows, 16)
    tq = rps * GRID_W
    nh = width // NA_HEAD_DIM
    qb0 = tok0 // tq
    sb0 = tok0 // s
    return pl.pallas_call(
        functools.partial(_natten_kernel, rows=rows, rows_per_step=rps),
        grid=(n_seq, heads, rows // rps),
        in_specs=[
            pl.BlockSpec((tq, NA_HEAD_DIM), lambda b, h, r: (qb0 + b * (rows // rps) + r, h)),
            pl.BlockSpec((s, NA_HEAD_DIM), lambda b, h, r: (sb0 + b, nh + h)),
            pl.BlockSpec((s, NA_HEAD_DIM), lambda b, h, r: (sb0 + b, 2 * nh + h)),
            pl.BlockSpec((1, NA_ROWS, GRID_W, NA_ROWS * GRID_W), lambda b, h, r: (h, 0, 0, 0)),
        ],
        out_specs=pl.BlockSpec((tq, NA_HEAD_DIM), lambda b, h, r: (b * (rows // rps) + r, h)),
        out_shape=jax.ShapeDtypeStruct((n_seq * s, width), BF16),
        compiler_params=_params(("arbitrary", "arbitrary", "arbitrary")),
        name="natten",
    )(proj, proj, proj, bias)


def _swa_kernel(sink_ref, q_ref, kp_ref, kc_ref, kn_ref, vp_ref, vc_ref, vn_ref, o_ref,
                *, seqs, kv_heads):
    n = pl.program_id(0)
    pos_blk = seqs.pos_tile(n, SW_BLOCK)
    n_blk = jnp.where(n < seqs.s0 // SW_BLOCK, seqs.s0 // SW_BLOCK, seqs.s1 // SW_BLOCK)
    is_first = pos_blk == 0
    is_last = pos_blk == n_blk - 1

    band = 3 * SW_BLOCK
    mpos = lax.broadcasted_iota(jnp.int32, (band, SW_BLOCK), 0)
    qpos = lax.broadcasted_iota(jnp.int32, (band, SW_BLOCK), 1)
    rel = mpos - SW_BLOCK - qpos
    valid = (jnp.abs(rel) <= SW_WINDOW)
    valid = valid & ((mpos >= SW_BLOCK) | jnp.logical_not(is_first))
    valid = valid & ((mpos < 2 * SW_BLOCK) | jnp.logical_not(is_last))

    kb = jnp.concatenate([kp_ref[...], kc_ref[...], kn_ref[...]], axis=0).astype(F32)
    vb = jnp.concatenate([vp_ref[...], vc_ref[...], vn_ref[...]], axis=0).astype(F32)
    vb_t = vb.T.astype(BF16)
    low = lax.broadcasted_iota(jnp.int32, (band, LANES), 1) < SW_HEAD_DIM
    qlane_low = lax.broadcasted_iota(jnp.int32, (SW_BLOCK, LANES), 1) < SW_HEAD_DIM

    for k in range(kv_heads):
        kpair = kb[:, (k // 2) * LANES:(k // 2 + 1) * LANES]
        kroll = pltpu.roll(kpair, SW_HEAD_DIM, 1)
        k2 = (jnp.where(low, kpair, kroll) if k % 2 == 0 else jnp.where(low, kroll, kpair))
        k2 = k2.astype(BF16)
        v_t = vb_t[k * SW_HEAD_DIM:(k + 1) * SW_HEAD_DIM, :]

        qs = []
        for g in range(SW_GROUP):
            hq = k * SW_GROUP + g
            q2 = q_ref[:, (hq // 2) * LANES:(hq // 2 + 1) * LANES]
            keep = qlane_low if hq % 2 == 0 else jnp.logical_not(qlane_low)
            qs.append(jnp.where(keep, q2, jnp.zeros_like(q2)))
        qm = jnp.concatenate(qs, axis=0)
        s_t = _dot_nt(k2, qm)
        for pp in range(SW_GROUP // 2):
            halves = []
            for g in (2 * pp, 2 * pp + 1):
                sink = sink_ref[k * SW_GROUP + g]
                s = jnp.where(valid, s_t[:, g * SW_BLOCK:(g + 1) * SW_BLOCK], NEG_INF)
                m = jnp.maximum(jnp.max(s, axis=0, keepdims=True), sink)
                e = jnp.exp(s - m)
                denom = jnp.sum(e, axis=0, keepdims=True) + jnp.exp(sink - m)
                halves.append(_dot(v_t, e.astype(BF16)) / denom)
            o_t = jnp.concatenate(halves, axis=0)
            c0 = (k * SW_GROUP // 2 + pp) * LANES
            o_ref[:, c0:c0 + LANES] = o_t.T.astype(BF16)


def _swa(qr, kr, proj, sinks_l, seqs, v_col, q_w, kv_w):
    t = qr.shape[0]
    nblk = t // SW_BLOCK
    kv_heads = kv_w // SW_HEAD_DIM
    assert kv_heads % 2 == 0 and kv_w % LANES == 0

    def prev(n, s):
        return jnp.maximum(n - 1, 0)

    def nxt(n, s):
        return jnp.minimum(n + 1, nblk - 1)

    vcb = v_col // kv_w
    grid_spec = pltpu.PrefetchScalarGridSpec(
        num_scalar_prefetch=1,
        grid=(nblk,),
        in_specs=[
            pl.BlockSpec((SW_BLOCK, q_w), lambda n, s: (n, 0)),
            pl.BlockSpec((SW_BLOCK, kv_w), lambda n, s: (prev(n, s), 0)),
            pl.BlockSpec((SW_BLOCK, kv_w), lambda n, s: (n, 0)),
            pl.BlockSpec((SW_BLOCK, kv_w), lambda n, s: (nxt(n, s), 0)),
            pl.BlockSpec((SW_BLOCK, kv_w), lambda n, s: (prev(n, s), vcb)),
            pl.BlockSpec((SW_BLOCK, kv_w), lambda n, s: (n, vcb)),
            pl.BlockSpec((SW_BLOCK, kv_w), lambda n, s: (nxt(n, s), vcb)),
        ],
        out_specs=pl.BlockSpec((SW_BLOCK, q_w), lambda n, s: (n, 0)),
    )
    return pl.pallas_call(
        functools.partial(_swa_kernel, seqs=seqs, kv_heads=kv_heads),
        grid_spec=grid_spec,
        out_shape=jax.ShapeDtypeStruct((t, q_w), BF16),
        compiler_params=_params(("arbitrary",)),
        name="swa",
    )(sinks_l.astype(F32), qr, kr, kr, kr, proj, proj, proj)


def _merge_kernel(h_ref, oa_ref, ob_ref, wga_ref, wgb_ref, wa_ref, wb_ref, o_ref):
    h = h_ref[...]
    ga = jax.nn.sigmoid(_dot(h, wga_ref[0]))
    gb = jax.nn.sigmoid(_dot(h, wgb_ref[0]))
    m = ga * _dot(oa_ref[...], wa_ref[0]) + gb * _dot(ob_ref[...], wb_ref[0])
    o_ref[...] = m.astype(BF16)


def _merge(h, oa, ob, wg_bf, wa_bf, wb_bf, layer):
    t, d = h.shape
    wa_w, wb_w = oa.shape[1], ob.shape[1]
    tm = _tile(t, 512)
    tn = _tile(d, 512)
    nj = d // tn
    return pl.pallas_call(
        _merge_kernel,
        grid=(t // tm, nj),
        in_specs=[
            pl.BlockSpec((tm, d), lambda i, j: (i, 0)),
            pl.BlockSpec((tm, wa_w), lambda i, j: (i, 0)),
            pl.BlockSpec((tm, wb_w), lambda i, j: (i, 0)),
            pl.BlockSpec((1, d, tn), lambda i, j: (layer, 0, j)),
            pl.BlockSpec((1, d, tn), lambda i, j: (layer, 0, nj + j)),
            pl.BlockSpec((1, wa_w, tn), lambda i, j: (layer, 0, j)),
            pl.BlockSpec((1, wb_w, tn), lambda i, j: (layer, 0, j)),
        ],
        out_specs=pl.BlockSpec((tm, tn), lambda i, j: (i, j)),
        out_shape=jax.ShapeDtypeStruct((t, d), BF16),
        compiler_params=_params(("arbitrary", "arbitrary")),
        name="merge",
    )(h, oa, ob, wg_bf, wg_bf, wa_bf, wb_bf)


def _matmul_kernel(x_ref, w_ref, o_ref):
    o_ref[...] = _dot(x_ref[...], w_ref[0]).astype(o_ref.dtype)


def _matmul(x, w, layer, tm_pref=512, tn_pref=1024):
    t, k = x.shape
    n = w.shape[2]
    tm, tn = _tile(t, tm_pref), _tile(n, tn_pref)
    return pl.pallas_call(
        _matmul_kernel,
        grid=(t // tm, n // tn),
        in_specs=[pl.BlockSpec((tm, k), lambda i, j: (i, 0)),
                  pl.BlockSpec((1, k, tn), lambda i, j: (layer, 0, j))],
        out_specs=pl.BlockSpec((tm, tn), lambda i, j: (i, j)),
        out_shape=jax.ShapeDtypeStruct((t, n), BF16),
        compiler_params=_params(("arbitrary", "arbitrary")),
        name="matmul",
    )(x, w)


def _rms(x, g):
    ms = jnp.mean(x * x, axis=-1, keepdims=True)
    return x * lax.rsqrt(ms + NORM_EPS) * g


def _postmix_kernel(x_ref, y_ref, gp_ref, gt_ref, gf_ref, sc_ref, sh_ref, wr_ref, br_ref,
                    x1_ref, h2_ref, idx_ref, wt_ref, *, n_experts):
    x1 = x_ref[...] + gt_ref[0] * _rms(y_ref[...].astype(F32), gp_ref[...])
    x1_ref[...] = x1
    h2 = _rms(x1, gf_ref[...]) * (1.0 + sc_ref[0]) + sh_ref[0]
    h2_ref[...] = _pack_halves(h2)
    h_hi = h2.astype(BF16)
    h_lo = (h2 - h_hi.astype(F32)).astype(BF16)
    hi_terms = _dot(h_hi, wr_ref[...])
    logits = (hi_terms[:, :LANES] + (hi_terms[:, LANES:] + _dot(h_lo, wr_ref[:, :LANES]))
              + br_ref[...])
    lane = lax.broadcasted_iota(jnp.int32, logits.shape, 1)
    work = jnp.where(lane < n_experts, logits, -jnp.inf)
    idx_out = jnp.zeros(logits.shape, jnp.int32)
    val_out = jnp.zeros(logits.shape, F32)
    vals = []
    for k in range(TOP_K):
        m = jnp.max(work, axis=-1, keepdims=True)
        sel = jnp.min(jnp.where(work == m, lane, LANES), axis=-1, keepdims=True)
        idx_out = jnp.where(lane == k, sel, idx_out)
        vals.append(m)
        work = jnp.where(lane == sel, -jnp.inf, work)
    es = [jnp.exp(v - vals[0]) for v in vals]
    tot = es[0]
    for e in es[1:]:
        tot = tot + e
    for k in range(TOP_K):
        val_out = jnp.where(lane == k, es[k] / tot, val_out)
    idx_ref[...] = idx_out
    wt_ref[...] = val_out


def _postmix(x, y, g_post, g_ffn, mod_rows, w_router, b_router, seqs, gt_idx, sc_idx, sh_idx):
    t, d = x.shape
    e = w_router.shape[1]
    tm = _tile(seqs.s1, 256)
    w_pad = jnp.zeros((d, LANES), F32).at[:, :e].set(w_router)
    w_hi = w_pad.astype(BF16)
    w_lo = (w_pad - w_hi.astype(F32)).astype(BF16)
    wr = jnp.concatenate([w_hi, w_lo], axis=1)
    br = jnp.zeros((1, LANES), F32).at[0, :e].set(b_router)
    mod_spec = lambda which: pl.BlockSpec(
        (1, 1, d), lambda i: (seqs.seq_of_tile(i, tm) * 6 + which, 0, 0))
    row = pl.BlockSpec((tm, d), lambda i: (i, 0))
    vec = pl.BlockSpec((1, d), lambda i: (0, 0))
    small = pl.BlockSpec((tm, LANES), lambda i: (i, 0))
    return pl.pallas_call(
        functools.partial(_postmix_kernel, n_experts=e),
        grid=(t // tm,),
        in_specs=[row, row, vec, mod_spec(gt_idx), vec, mod_spec(sc_idx), mod_spec(sh_idx),
                  pl.BlockSpec((d, 2 * LANES), lambda i: (0, 0)),
                  pl.BlockSpec((1, LANES), lambda i: (0, 0))],
        out_specs=[row, pl.BlockSpec((tm, d // 2), lambda i: (i, 0)), small, small],
        out_shape=[jax.ShapeDtypeStruct((t, d), F32), jax.ShapeDtypeStruct((t, d // 2), jnp.uint32),
                   jax.ShapeDtypeStruct((t, LANES), jnp.int32),
                   jax.ShapeDtypeStruct((t, LANES), F32)],
        compiler_params=_params(("arbitrary",)),
        name="postmix",
    )(x, y, g_post.reshape(1, d), mod_rows, g_ffn.reshape(1, d), mod_rows, mod_rows, wr, br)


def _rank_kernel(idx_ref, rank_ref, cnt_ref, carry):
    @pl.when(pl.program_id(0) == 0)
    def _():
        carry[...] = jnp.zeros_like(carry)

    idx = idx_ref[...]
    tb = idx.shape[0]
    lane = lax.broadcasted_iota(jnp.int32, idx.shape, 1)
    hits = [lane == idx[:, k:k + 1] for k in range(TOP_K)]
    onehot = jnp.zeros(idx.shape, F32)
    for hit in hits:
        onehot = onehot + hit.astype(F32)
    r = lax.broadcasted_iota(jnp.int32, (tb, tb), 0)
    c = lax.broadcasted_iota(jnp.int32, (tb, tb), 1)
    tri = (c < r).astype(BF16)
    before = _dot(tri, onehot.astype(BF16)) + carry[...]
    rank = jnp.zeros(idx.shape, F32)
    for k, hit in enumerate(hits):
        rk = jnp.sum(jnp.where(hit, before, 0.0), axis=-1, keepdims=True)
        rank = jnp.where(lane == k, rk, rank)
    rank_ref[...] = rank.astype(jnp.int32)
    carry[...] = carry[...] + jnp.sum(onehot, axis=0, keepdims=True)
    cnt_ref[...] = jnp.broadcast_to(carry[...], cnt_ref.shape).astype(jnp.int32)


def _ranks(idx):
    t = idx.shape[0]
    tb = _tile(t, 256)
    return pl.pallas_call(
        _rank_kernel,
        grid=(t // tb,),
        in_specs=[pl.BlockSpec((tb, LANES), lambda i: (i, 0))],
        out_specs=[pl.BlockSpec((tb, LANES), lambda i: (i, 0)),
                   pl.BlockSpec((SUBLANES, LANES), lambda i: (0, 0))],
        out_shape=[jax.ShapeDtypeStruct((t, LANES), jnp.int32),
                   jax.ShapeDtypeStruct((SUBLANES, LANES), jnp.int32)],
        scratch_shapes=[pltpu.VMEM((1, LANES), F32)],
        compiler_params=_params(("arbitrary",)),
        name="ranks",
    )(idx)


def _row_copy(src_hbm, row, dst, slot, sem):
    return pltpu.make_async_copy(src_hbm.at[pl.ds(row, 1), :], dst.at[pl.ds(slot, 1), :], sem)


def _wait_rows(src_hbm, dst, sem, n):
    def body(c, carry):
        for _ in range(DMA_UNROLL):
            _row_copy(src_hbm, 0, dst, 0, sem).wait()
        return carry

    lax.fori_loop(0, n // DMA_UNROLL, body, 0)


def _dispatch_kernel(dest_ref, zstart_ref, used_ref, h_ref, xs_hbm, zero_sc, sem, zsem,
                     *, n_exp, tm):
    i = pl.program_id(0)
    tc = h_ref.shape[0]

    @pl.when(i == 0)
    def _():
        zero_sc[...] = jnp.zeros_like(zero_sc)

        def zero_tile(first_row):
            start = pl.multiple_of(first_row, tm)
            cp = pltpu.make_async_copy(zero_sc, xs_hbm.at[pl.ds(start, tm), :], zsem)
            cp.start()
            cp.wait()

        for e in range(n_exp):
            zero_tile(zstart_ref[e])

        def tail(r, carry):
            zero_tile(r * tm)
            return carry

        lax.fori_loop(used_ref[0], xs_hbm.shape[0] // tm, tail, 0)

    def body(c, carry):
        for u in range(DMA_UNROLL):
            t = c * DMA_UNROLL + u
            for k in range(TOP_K):
                pltpu.make_async_copy(h_ref.at[pl.ds(t, 1), :],
                                      xs_hbm.at[pl.ds(dest_ref[0, k, t], 1), :],
                                      sem).start(priority=k % 2)
        return carry

    lax.fori_loop(0, tc // DMA_UNROLL, body, 0)

    def drain(c, carry):
        for _ in range(DMA_UNROLL * TOP_K):
            pltpu.make_async_copy(h_ref.at[pl.ds(0, 1), :], xs_hbm.at[pl.ds(0, 1), :], sem).wait()
        return carry

    lax.fori_loop(0, tc // DMA_UNROLL, drain, 0)


def _dispatch(h2p, dest, pad_ends, tiles_used, n_rows, tm):
    t, half = h2p.shape
    n_exp = pad_ends.shape[0]
    tc = _tile(t, 256)
    assert tc % DMA_UNROLL == 0
    nt = t // tc
    dest_t = dest.reshape(nt, tc, TOP_K).transpose(0, 2, 1)
    zstart = jnp.maximum(pad_ends - tm, 0).astype(jnp.int32)
    grid_spec = pltpu.PrefetchScalarGridSpec(
        num_scalar_prefetch=0,
        grid=(nt,),
        in_specs=[
            pl.BlockSpec((1, TOP_K, tc), lambda i: (i, 0, 0), memory_space=pltpu.SMEM),
            pl.BlockSpec(memory_space=pltpu.SMEM),
            pl.BlockSpec(memory_space=pltpu.SMEM),
            pl.BlockSpec((tc, half), lambda i: (i, 0)),
        ],
        out_specs=pl.BlockSpec(memory_space=pl.ANY),
        scratch_shapes=[pltpu.VMEM((tm, half), jnp.uint32),
                        pltpu.SemaphoreType.DMA(()), pltpu.SemaphoreType.DMA(())],
    )
    return pl.pallas_call(
        functools.partial(_dispatch_kernel, n_exp=n_exp, tm=tm),
        grid_spec=grid_spec,
        out_shape=jax.ShapeDtypeStruct((n_rows, half), jnp.uint32),
        compiler_params=_params(("arbitrary",)),
        name="dispatch",
    )(dest_t, zstart, tiles_used, h2p)


def _cast_weights_on_expert_change(e_ref, r, used, w_refs, w_sc):
    changed = (r == 0) | (e_ref[r] != e_ref[jnp.maximum(r - 1, 0)])

    @pl.when(changed & (r < used))
    def _():
        k = w_refs[0].shape[2]
        rows = min(CAST_ROWS, k)

        def body(c, carry):
            sl = pl.ds(pl.multiple_of(c * rows, rows), rows)
            for i, w_ref in enumerate(w_refs):
                w_sc[i, sl, :] = w_ref[0, 0, sl, :].astype(BF16)
            return carry

        lax.fori_loop(0, k // rows, body, 0)


def _gateup_kernel(e_ref, src_ref, used_ref, x_ref, wg_ref, wu_ref, bg_ref, bu_ref, o_ref, w_sc):
    r = pl.program_id(1)
    used = used_ref[0]
    _cast_weights_on_expert_change(e_ref, r, used, (wg_ref, wu_ref), w_sc)

    @pl.when(r < used)
    def _():
        lo, hi = _unpack_halves(x_ref[...])
        x = jnp.concatenate([lo.astype(BF16), hi.astype(BF16)], axis=1)
        gate = _dot(x, w_sc[0]) + bg_ref[0, 0]
        up = _dot(x, w_sc[1]) + bu_ref[0, 0]
        gate = jnp.minimum(gate, SWIGLU_LIMIT)
        up = jnp.clip(up, -SWIGLU_LIMIT, SWIGLU_LIMIT)
        glu = gate * jax.nn.sigmoid(gate * SWIGLU_ALPHA)
        o_ref[...] = ((up + 1.0) * glu).astype(BF16)

    @pl.when(r >= used)
    def _():
        o_ref[...] = jnp.zeros_like(o_ref)


def _gateup(xsp, w_gate_up, b_gate_up, layer, tile_expert, tile_src, tiles_used, tm):
    n_rows, half = xsp.shape
    _, n_exp, d, two_f = w_gate_up.shape
    f = two_f // 2
    tn = _tile(f, 512)
    nj = f // tn
    nt = n_rows // tm
    b4 = b_gate_up.reshape(-1, n_exp, 1, two_f)
    grid_spec = pltpu.PrefetchScalarGridSpec(
        num_scalar_prefetch=3,
        grid=(nj, nt),
        in_specs=[
            pl.BlockSpec((tm, half), lambda j, r, e, s, u: (s[r], 0)),
            pl.BlockSpec((1, 1, d, tn), lambda j, r, e, s, u: (layer, e[r], 0, j)),
            pl.BlockSpec((1, 1, d, tn), lambda j, r, e, s, u: (layer, e[r], 0, nj + j)),
            pl.BlockSpec((1, 1, 1, tn), lambda j, r, e, s, u: (layer, e[r], 0, j)),
            pl.BlockSpec((1, 1, 1, tn), lambda j, r, e, s, u: (layer, e[r], 0, nj + j)),
        ],
        out_specs=pl.BlockSpec((tm, tn), lambda j, r, e, s, u: (r, j)),
        scratch_shapes=[pltpu.VMEM((2, d, tn), BF16)],
    )
    return pl.pallas_call(
        _gateup_kernel,
        grid_spec=grid_spec,
        out_shape=jax.ShapeDtypeStruct((n_rows, f), BF16),
        compiler_params=_params(("arbitrary", "arbitrary")),
        name="gateup",
    )(tile_expert, tile_src, tiles_used, xsp, w_gate_up, w_gate_up, b4, b4)


def _down_kernel(e_ref, src_ref, used_ref, a_ref, wl_ref, wh_ref, bl_ref, bh_ref, o_ref, w_sc):
    r = pl.program_id(1)
    used = used_ref[0]
    _cast_weights_on_expert_change(e_ref, r, used, (wl_ref, wh_ref), w_sc)

    @pl.when(r < used)
    def _():
        a = a_ref[...]
        lo = _dot(a, w_sc[0]) + bl_ref[0, 0]
        hi = _dot(a, w_sc[1]) + bh_ref[0, 0]
        o_ref[...] = _pack_pair(lo, hi)

    @pl.when(r >= used)
    def _():
        o_ref[...] = jnp.zeros_like(o_ref)


def _down(a, w_down, b_down, layer, tile_expert, tile_src, tiles_used, tm):
    n_rows, f = a.shape
    _, n_exp, _, d = w_down.shape
    half = d // 2
    tn = _tile(half, 1024)
    nj = half // tn
    nt = n_rows // tm
    b4 = b_down.reshape(-1, n_exp, 1, d)
    grid_spec = pltpu.PrefetchScalarGridSpec(
        num_scalar_prefetch=3,
        grid=(nj, nt),
        in_specs=[
            pl.BlockSpec((tm, f), lambda j, r, e, s, u: (s[r], 0)),
            pl.BlockSpec((1, 1, f, tn), lambda j, r, e, s, u: (layer, e[r], 0, j)),
            pl.BlockSpec((1, 1, f, tn), lambda j, r, e, s, u: (layer, e[r], 0, nj + j)),
            pl.BlockSpec((1, 1, 1, tn), lambda j, r, e, s, u: (layer, e[r], 0, j)),
            pl.BlockSpec((1, 1, 1, tn), lambda j, r, e, s, u: (layer, e[r], 0, nj + j)),
        ],
        out_specs=pl.BlockSpec((tm, tn), lambda j, r, e, s, u: (r, j)),
        scratch_shapes=[pltpu.VMEM((2, f, tn), BF16)],
    )
    return pl.pallas_call(
        _down_kernel,
        grid_spec=grid_spec,
        out_shape=jax.ShapeDtypeStruct((n_rows, half), jnp.uint32),
        compiler_params=_params(("arbitrary", "arbitrary")),
        name="down",
    )(tile_expert, tile_src, tiles_used, a, w_down, w_down, b4, b4)


def _combine_kernel(dest_ref, nxt_ref, ys_hbm, wt_ref, x_ref, g_ref, gt_ref, o_ref, buf, sem,
                    *, n):
    i = pl.program_id(0)
    tc = x_ref.shape[0]
    half = buf.shape[3]

    def start(idx_ref, slot):
        def body(c, carry):
            for u in range(DMA_UNROLL):
                t = c * DMA_UNROLL + u
                for k in range(TOP_K):
                    _row_copy(ys_hbm, idx_ref[0, k, t], buf.at[slot, k], t,
                              sem.at[slot]).start(priority=k % 2)
            return carry

        lax.fori_loop(0, tc // DMA_UNROLL, body, 0)

    @pl.when(i == 0)
    def _():
        start(dest_ref, 0)

    @pl.when(i + 1 < n)
    def _():
        start(nxt_ref, (i + 1) % 2)

    slot = i % 2
    for k in range(TOP_K):
        _wait_rows(ys_hbm, buf.at[slot, k], sem.at[slot], tc)
    wt = wt_ref[...]
    y_lo = jnp.zeros((tc, half), F32)
    y_hi = jnp.zeros((tc, half), F32)
    for k in range(TOP_K):
        lo, hi = _unpack_halves(buf[slot, k])
        y_lo = y_lo + wt[:, k:k + 1] * lo
        y_hi = y_hi + wt[:, k:k + 1] * hi
    y = jnp.concatenate([y_lo, y_hi], axis=1)
    o_ref[...] = x_ref[...] + gt_ref[0] * _rms(y, g_ref[...])


def _combine(ysp, dest, wt, x1, g_post, mod_rows, seqs, gt_idx, tok0, n_tok):
    t, d = x1.shape
    tc = _tile(seqs.s1, 128)
    assert tc % DMA_UNROLL == 0 and tok0 % tc == 0 and n_tok % tc == 0
    nt = n_tok // tc
    t0 = tok0 // tc
    dest_t = dest.reshape(t // tc, tc, TOP_K).transpose(0, 2, 1)
    grid_spec = pltpu.PrefetchScalarGridSpec(
        num_scalar_prefetch=0,
        grid=(nt,),
        in_specs=[
            pl.BlockSpec((1, TOP_K, tc), lambda i: (t0 + i, 0, 0), memory_space=pltpu.SMEM),
            pl.BlockSpec((1, TOP_K, tc), lambda i: (t0 + jnp.minimum(i + 1, nt - 1), 0, 0),
                         memory_space=pltpu.SMEM),
            pl.BlockSpec(memory_space=pl.ANY),
            pl.BlockSpec((tc, LANES), lambda i: (t0 + i, 0)),
            pl.BlockSpec((tc, d), lambda i: (t0 + i, 0)),
            pl.BlockSpec((1, d), lambda i: (0, 0)),
            pl.BlockSpec((1, 1, d),
                         lambda i: (seqs.seq_of_tile(t0 + i, tc) * 6 + gt_idx, 0, 0)),
        ],
        out_specs=pl.BlockSpec((tc, d), lambda i: (i, 0)),
        scratch_shapes=[pltpu.VMEM((2, TOP_K, tc, d // 2), jnp.uint32),
                        pltpu.SemaphoreType.DMA((2,))],
    )
    return pl.pallas_call(
        functools.partial(_combine_kernel, n=nt),
        grid_spec=grid_spec,
        out_shape=jax.ShapeDtypeStruct((n_tok, d), F32),
        compiler_params=_params(("arbitrary",)),
        name="combine",
    )(dest_t, dest_t, ysp, wt, x1, g_post.reshape(1, d), mod_rows)


def _moe(h2p, idx, w_gate_up, b_gate_up, w_down, b_down, layer, tm):
    t = h2p.shape[0]
    n_exp = w_gate_up.shape[1]
    rank, cnt = _ranks(idx)
    counts = cnt[0, :n_exp]
    padded = (counts + tm - 1) // tm * tm
    pad_ends = jnp.cumsum(padded)
    pad_starts = pad_ends - padded
    n_assign = t * TOP_K
    nt = -(-n_assign // tm) + n_exp
    n_rows = nt * tm
    top_idx = idx[:, :TOP_K]
    dest = (pad_starts[top_idx] + rank[:, :TOP_K]).astype(jnp.int32)
    tiles_used = (pad_ends[-1] // tm).astype(jnp.int32).reshape(1)
    tile_ids = jnp.arange(nt, dtype=jnp.int32)
    tile_src = jnp.minimum(tile_ids, tiles_used[0] - 1)
    tile_expert = jnp.sum(pad_ends[None, :] <= (tile_src * tm)[:, None], axis=1)
    tile_expert = jnp.minimum(tile_expert, n_exp - 1).astype(jnp.int32)

    xsp = _dispatch(h2p, dest, pad_ends, tiles_used, n_rows, tm)
    a = _gateup(xsp, w_gate_up, b_gate_up, layer, tile_expert, tile_src, tiles_used, tm)
    ysp = _down(a, w_down, b_down, layer, tile_expert, tile_src, tiles_used, tm)
    return ysp, dest


def kernel(x_prompt, x_sample, c_prompt, c_sample, w_ada, b_ada, g_mix_pre, g_mix_post, g_ffn_pre, g_ffn_post, w_in, w_branch_gate, rpb, sinks, w_proj_a, w_proj_b, w_o, w_router, b_router, w_gate_up, b_gate_up, w_down, b_down):
    b0, s0, d = x_prompt.shape
    b1, s1, _ = x_sample.shape
    assert b0 == 1 and s0 % s1 == 0
    depth = w_ada.shape[0]
    seqs = _Seqs(s0, b1, s1)
    na_w = w_proj_a.shape[1]
    sw_w = w_proj_b.shape[1]
    kv_w = (w_in.shape[2] - 3 * na_w - sw_w) // 2
    na_heads = na_w // NA_HEAD_DIM
    q_col, k_col, v_col = 3 * na_w, 3 * na_w + sw_w, 3 * na_w + sw_w + kv_w
    moe_tm = _tile(seqs.total * TOP_K, 512)

    x = jnp.concatenate([x_prompt.reshape(s0, d), x_sample.reshape(b1 * s1, d)], axis=0)
    c8 = jnp.zeros((SUBLANES, d), F32).at[:1 + b1].set(
        jnp.concatenate([c_prompt, c_sample], axis=0))
    mod = _adaln(c8, w_ada, b_ada)
    cos_t, sin_t = _rope_tables(s0)
    w_in_bf = w_in.astype(BF16)
    wg_bf = w_branch_gate.astype(BF16)
    wa_bf = w_proj_a.astype(BF16)
    wb_bf = w_proj_b.astype(BF16)
    wo_bf = w_o.astype(BF16)

    for l in range(depth):
        mod_rows = mod[l].reshape(SUBLANES * 6, 1, d)
        h = _prenorm(x, g_mix_pre[l], mod_rows, seqs, 1, 0)
        proj = _matmul(h, w_in_bf, l, tm_pref=1024, tn_pref=512)
        bias = _natten_bias(rpb[l])
        oa = jnp.concatenate([
            _natten(proj, bias, 0, 1, s0, na_w, na_heads),
            _natten(proj, bias, s0, b1, s1, na_w, na_heads)], axis=0)
        qr, kr = _rope(proj, cos_t, sin_t, seqs, q_col, sw_w, k_col, kv_w)
        ob = _swa(qr, kr, proj, sinks[l], seqs, v_col, sw_w, kv_w)
        merged = _merge(h, oa, ob, wg_bf, wa_bf, wb_bf, l)
        y = _matmul(merged, wo_bf, l)
        x1, h2p, idx, wt = _postmix(x, y, g_mix_post[l], g_ffn_pre[l], mod_rows,
                                    w_router[l], b_router[l], seqs, 2, 4, 3)
        ysp, dest = _moe(h2p, idx, w_gate_up, b_gate_up, w_down, b_down, l, moe_tm)
        combine = functools.partial(_combine, ysp, dest, wt, x1, g_ffn_post[l], mod_rows, seqs, 5)
        if l + 1 < depth:
            x = combine(0, seqs.total)
        else:
            y_prompt = combine(0, s0)
            y_sample = combine(s0, b1 * s1)

    return (y_prompt.reshape(b0, s0, d), y_sample.reshape(b1, s1, d))
```
